```python
import math
import jax, jax.numpy as jnp
from jax import lax
import numpy as np

D_MODEL = 4096
BATCH = 2
SEQ = 4096
DEPTH = 2

MIX_WIDTH = D_MODEL
ATTN_WIDTH = MIX_WIDTH // 2
GMLP_WIDTH = MIX_WIDTH - ATTN_WIDTH
DIFF_HEAD_DIM = 128
DIFF_V_DIM = 2 * DIFF_HEAD_DIM
N_DIFF_HEADS = ATTN_WIDTH // DIFF_V_DIM
N_GMLP_GROUPS = 8
GMLP_GROUP_DIM = GMLP_WIDTH // N_GMLP_GROUPS
CHUNK = 128
Q_BLOCK = 128
D_FF = 4 * D_MODEL
N_BUCKETS = 32
MAX_DISTANCE = 128
NORM_EPS = 1e-6
N_MOD = 6
IN_COLS = 3 * ATTN_WIDTH + 2 * GMLP_WIDTH

kernel_name = "hybrid_diffattn_gmlp_parallel_block"


def rms_norm(x, g):
    xf = x.astype(jnp.float32)
    y = xf * lax.rsqrt(jnp.mean(xf * xf, axis=-1, keepdims=True) + NORM_EPS)
    return (y * g.astype(jnp.float32)).astype(x.dtype)


def layer_norm(x, g, b):
    xf = x.astype(jnp.float32)
    mu = jnp.mean(xf, axis=-1, keepdims=True)
    var = jnp.mean(jnp.square(xf - mu), axis=-1, keepdims=True)
    y = (xf - mu) * lax.rsqrt(var + NORM_EPS)
    return (y * g.astype(jnp.float32) + b.astype(jnp.float32)).astype(x.dtype)


def t5_bucket(rel):
    n = jnp.maximum(-rel, 0)
    max_exact = N_BUCKETS // 2
    is_small = n < max_exact
    nf = jnp.maximum(n, 1).astype(jnp.float32)
    large = max_exact + (jnp.log(nf / max_exact) / math.log(MAX_DISTANCE / max_exact)
                         * (N_BUCKETS - max_exact)).astype(jnp.int32)
    large = jnp.minimum(large, N_BUCKETS - 1)
    return jnp.where(is_small, n, large)


def diff_attention(q, k, v, rel_bias, lam):
    B, S = q.shape[0], q.shape[1]
    nb = S // Q_BLOCK
    qb_all = q.reshape(B, nb, Q_BLOCK, N_DIFF_HEADS, 2, DIFF_HEAD_DIM).transpose(1, 0, 3, 4, 2, 5)
    kt = k.transpose(0, 2, 3, 1, 4)
    vt = v.transpose(0, 2, 1, 3)
    k_pos = jnp.arange(S)
    scale = DIFF_HEAD_DIM ** -0.5

    def block(args):
        qb, bi = args
        q_pos = bi * Q_BLOCK + jnp.arange(Q_BLOCK)
        rel = k_pos[None, :] - q_pos[:, None]
        bias = jnp.transpose(rel_bias[t5_bucket(rel)], (2, 0, 1)).astype(jnp.float32)
        s = jnp.einsum('bhcqd,bhckd->bhcqk', qb, kt).astype(jnp.float32) * scale
        s = s + bias[None, :, None]
        s = jnp.where((rel <= 0)[None, None, None], s, -jnp.inf)
        p = jax.nn.softmax(s, axis=-1)
        a = p[:, :, 0] - lam * p[:, :, 1]
        return jnp.einsum('bhqk,bhkv->bhqv', a.astype(vt.dtype), vt)

    out = lax.map(block, (qb_all, jnp.arange(nb)))
    return out.transpose(1, 0, 3, 2, 4).reshape(B, S, N_DIFF_HEADS, DIFF_V_DIM)


def spatial_gating(z, w_s, b_s, v_g, v_b):
    B, S = z.shape[0], z.shape[1]
    u, v = jnp.split(z, 2, axis=-1)
    v = layer_norm(v, v_g, v_b)
    nc = S // CHUNK
    vg = v.reshape(B, nc, CHUNK, N_GMLP_GROUPS, GMLP_GROUP_DIM)
    ws = w_s * jnp.tril(jnp.ones((CHUNK, CHUNK), dtype=w_s.dtype))[None]
    mixed = jnp.einsum('gts,bnsgc->bntgc', ws, vg) + b_s.T[None, None, :, :, None]
    return u * mixed.reshape(B, S, GMLP_WIDTH)


def setup_inputs(seed: int = 0) -> dict:
    key = jax.random.key(seed)
    ks = jax.random.split(key, 24)
    f32 = jnp.float32
    L, D = DEPTH, D_MODEL

    def nrm(k, shape, s):
        return jax.random.normal(k, shape, f32) * s

    def gain(k, shape):
        return 1.0 + 0.05 * jax.random.normal(k, shape, f32)

    return {
        "x": jax.random.normal(ks[0], (BATCH, SEQ, D), f32),
        "c": jax.random.normal(ks[1], (BATCH, D), f32),
        "rel_bias": nrm(ks[2], (N_BUCKETS, N_DIFF_HEADS), 0.5),
        "w_ada": nrm(ks[3], (L, D, N_MOD * D), 0.5 * D ** -0.5),
        "b_ada": nrm(ks[4], (L, N_MOD * D), 0.01),
        "pre_mix_g": gain(ks[5], (L, D)),
        "w_in": nrm(ks[6], (L, D, IN_COLS), D ** -0.5),
        "lambda_q1": nrm(ks[7], (L, DIFF_HEAD_DIM), 0.1),
        "lambda_k1": nrm(ks[8], (L, DIFF_HEAD_DIM), 0.1),
        "lambda_q2": nrm(ks[9], (L, DIFF_HEAD_DIM), 0.1),
        "lambda_k2": nrm(ks[10], (L, DIFF_HEAD_DIM), 0.1),
        "subln_g": gain(ks[11], (L, DIFF_V_DIM)),
        "v_norm_g": gain(ks[12], (L, GMLP_WIDTH)),
        "v_norm_b": nrm(ks[13], (L, GMLP_WIDTH), 0.02),
        "w_s": nrm(ks[14], (L, N_GMLP_GROUPS, CHUNK, CHUNK), CHUNK ** -0.5),
        "b_s": gain(ks[15], (L, N_GMLP_GROUPS, CHUNK)),
        "w_out": nrm(ks[16], (L, MIX_WIDTH, D), MIX_WIDTH ** -0.5),
        "post_mix_g": gain(ks[17], (L, D)),
        "pre_mlp_g": gain(ks[18], (L, D)),
        "w_1": nrm(ks[19], (L, D, D_FF), D ** -0.5),
        "w_2": nrm(ks[20], (L, D_FF, D), D_FF ** -0.5),
        "post_mlp_g": gain(ks[21], (L, D)),
    }


def reference(x, c, rel_bias, w_ada, b_ada, pre_mix_g, w_in, lambda_q1, lambda_k1,
              lambda_q2, lambda_k2, subln_g, v_norm_g, v_norm_b, w_s, b_s, w_out,
              post_mix_g, pre_mlp_g, w_1, w_2, post_mlp_g):
    B, S = x.shape[0], x.shape[1]
    c_act = jax.nn.silu(c)
    for l in range(DEPTH):
        lambda_init = 0.8 - 0.6 * math.exp(-0.3 * l)
        mod = c_act @ w_ada[l] + b_ada[l]
        sh_a, sc_a, g_a, sh_m, sc_m, g_m = [m[:, None, :] for m in jnp.split(mod, N_MOD, axis=-1)]

        h = rms_norm(x, pre_mix_g[l]) * (1 + sc_a) + sh_a
        proj = h @ w_in[l]
        q, k, v, z = jnp.split(proj, [ATTN_WIDTH, 2 * ATTN_WIDTH, 3 * ATTN_WIDTH], axis=-1)
        q = q.reshape(B, S, N_DIFF_HEADS, 2, DIFF_HEAD_DIM)
        k = k.reshape(B, S, N_DIFF_HEADS, 2, DIFF_HEAD_DIM)
        v = v.reshape(B, S, N_DIFF_HEADS, DIFF_V_DIM)
        lam = (jnp.exp(jnp.sum(lambda_q1[l].astype(jnp.float32) * lambda_k1[l].astype(jnp.float32)))
               - jnp.exp(jnp.sum(lambda_q2[l].astype(jnp.float32) * lambda_k2[l].astype(jnp.float32)))
               + lambda_init)
        attn = diff_attention(q, k, v, rel_bias, lam)
        attn = (rms_norm(attn, subln_g[l]) * (1 - lambda_init)).reshape(B, S, ATTN_WIDTH)
        gm = spatial_gating(jax.nn.gelu(z, approximate=False), w_s[l], b_s[l],
                            v_norm_g[l], v_norm_b[l])
        y = jnp.concatenate([attn, gm], axis=-1) @ w_out[l]
        x = x + g_a * rms_norm(y, post_mix_g[l])

        h = rms_norm(x, pre_mlp_g[l]) * (1 + sc_m) + sh_m
        y = jnp.square(jax.nn.relu(h @ w_1[l])) @ w_2[l]
        x = x + g_m * rms_norm(y, post_mlp_g[l])
    return x
```

```python
import functools
import math

import jax
import jax.numpy as jnp
from jax import lax
from jax.experimental import pallas as pl
from jax.experimental.pallas import tpu as pltpu

F32 = jnp.float32
BF16 = jnp.bfloat16

D_MODEL = 4096
DEPTH = 2
ATTN_WIDTH = 2048
GMLP_WIDTH = 2048
HEAD_DIM = 128
V_DIM = 2 * HEAD_DIM
N_HEADS = ATTN_WIDTH // V_DIM
N_GROUPS = 8
GROUP_DIM = GMLP_WIDTH // N_GROUPS
CHUNK = 128
D_FF = 4 * D_MODEL
N_BUCKETS = 32
MAX_DISTANCE = 128
NORM_EPS = 1e-6
N_MOD = 6
QKV_COLS = 3 * ATTN_WIDTH
IN_COLS = QKV_COLS + 2 * GMLP_WIDTH

V7X_VMEM_BYTES = 64 * 1024 * 1024
VMEM_LIMIT_BYTES = V7X_VMEM_BYTES - 8 * 1024 * 1024
SUBLANES = 8

ATTN_BLOCK = 256
ROW_BLOCK = 256


def _params(semantics):
    return pltpu.CompilerParams(dimension_semantics=semantics,
                                vmem_limit_bytes=VMEM_LIMIT_BYTES)


def _ada_kernel(c_ref, w_ref, b_ref, o_ref):
    c = c_ref[...]
    c_act = (c * jax.nn.sigmoid(c)).astype(BF16)
    w = w_ref[0].astype(BF16)
    o_ref[0] = jnp.dot(c_act, w, preferred_element_type=F32) + b_ref[0]


def _ada_mod(c, w_ada, b_ada):
    B, D = c.shape
    L, _, N = w_ada.shape
    tn = 512
    c_pad = jnp.zeros((SUBLANES, D), F32).at[:B].set(c)
    out = pl.pallas_call(
        _ada_kernel,
        grid=(L, N // tn),
        in_specs=[
            pl.BlockSpec((SUBLANES, D), lambda l, j: (0, 0)),
            pl.BlockSpec((1, D, tn), lambda l, j: (l, 0, j)),
            pl.BlockSpec((1, 1, tn), lambda l, j: (l, 0, j)),
        ],
        out_specs=pl.BlockSpec((1, SUBLANES, tn), lambda l, j: (l, 0, j)),
        out_shape=jax.ShapeDtypeStruct((L, SUBLANES, N), F32),
        compiler_params=_params(("arbitrary", "arbitrary")),
        name="ada_mod",
    )(c_pad, w_ada, b_ada.reshape(L, 1, N))
    return out[:, :B].reshape(L, B, N_MOD, D)


def _rms(x, g):
    ms = jnp.mean(x * x, axis=-1, keepdims=True)
    return x * lax.rsqrt(ms + NORM_EPS) * g


def _prenorm_kernel(x_ref, g_ref, mod_ref, h_ref, *, shift_idx):
    shift = mod_ref[0, shift_idx:shift_idx + 1, :]
    scale = mod_ref[0, shift_idx + 1:shift_idx + 2, :]
    h = _rms(x_ref[...], g_ref[...]) * (1.0 + scale) + shift
    h_ref[...] = h.astype(BF16)


def _prenorm(x2, g, mod, shift_idx, rows_per_batch):
    M, D = x2.shape
    tm = ROW_BLOCK
    bpb = rows_per_batch // tm
    return pl.pallas_call(
        functools.partial(_prenorm_kernel, shift_idx=shift_idx),
        grid=(M // tm,),
        in_specs=[
            pl.BlockSpec((tm, D), lambda i: (i, 0)),
            pl.BlockSpec((1, D), lambda i: (0, 0)),
            pl.BlockSpec((1, N_MOD, D), lambda i: (i // bpb, 0, 0)),
        ],
        out_specs=pl.BlockSpec((tm, D), lambda i: (i, 0)),
        out_shape=jax.ShapeDtypeStruct((M, D), BF16),
        compiler_params=_params(("arbitrary",)),
        name="prenorm",
    )(x2, g.reshape(1, D), mod)


def _postnorm_kernel(y_ref, x_ref, pg_ref, mod_ref, ng_ref, xo_ref, h_ref=None, *,
                     gate_idx, next_shift_idx):
    gate = mod_ref[0, gate_idx:gate_idx + 1, :]
    x_new = x_ref[...] + gate * _rms(y_ref[...], pg_ref[...])
    xo_ref[...] = x_new
    if next_shift_idx is not None:
        shift = mod_ref[1, next_shift_idx:next_shift_idx + 1, :]
        scale = mod_ref[1, next_shift_idx + 1:next_shift_idx + 2, :]
        h = _rms(x_new, ng_ref[...]) * (1.0 + scale) + shift
        h_ref[...] = h.astype(BF16)


def _postnorm(y2, x2, post_g, mod_pair, next_g, gate_idx, next_shift_idx, rows_per_batch):
    M, D = x2.shape
    tm = ROW_BLOCK
    bpb = rows_per_batch // tm
    row_spec = pl.BlockSpec((tm, D), lambda i: (i, 0))
    vec_spec = pl.BlockSpec((1, D), lambda i: (0, 0))
    with_next = next_shift_idx is not None
    kern = functools.partial(_postnorm_kernel, gate_idx=gate_idx,
                             next_shift_idx=next_shift_idx)
    out_shape = [jax.ShapeDtypeStruct((M, D), F32)]
    out_specs = [row_spec]
    if with_next:
        out_shape.append(jax.ShapeDtypeStruct((M, D), BF16))
        out_specs.append(row_spec)
    res = pl.pallas_call(
        kern,
        grid=(M // tm,),
        in_specs=[
            row_spec, row_spec, vec_spec,
            pl.BlockSpec((None, 2, N_MOD, D), lambda i: (i // bpb, 0, 0, 0)),
            vec_spec,
        ],
        out_specs=out_specs,
        out_shape=out_shape,
        compiler_params=_params(("arbitrary",)),
        name="postnorm",
    )(y2, x2, post_g.reshape(1, D), mod_pair, next_g.reshape(1, D))
    return (res[0], res[1]) if with_next else (res[0], None)


def _matmul_kernel(*refs, n_lhs, k_sizes, n_k, epilogue):
    lhs_refs = refs[:n_lhs]
    w_ref = refs[n_lhs]
    o_ref = refs[n_lhs + 1]
    acc = None
    off = 0
    for a_ref, ks in zip(lhs_refs, k_sizes):
        part = jnp.dot(a_ref[...], w_ref[off:off + ks, :], preferred_element_type=F32)
        acc = part if acc is None else acc + part
        off += ks

    def finish(r):
        if epilogue == "relu2":
            r = jnp.square(jnp.maximum(r, 0.0))
        o_ref[...] = r.astype(o_ref.dtype)

    if n_k == 1:
        finish(acc)
    else:
        acc_ref = refs[n_lhs + 2]
        k = pl.program_id(2)

        @pl.when(k == 0)
        def _():
            acc_ref[...] = acc

        @pl.when(k > 0)
        def _():
            acc_ref[...] += acc

        @pl.when(k == n_k - 1)
        def _():
            finish(acc_ref[...])


def _matmul(lhs_list, w, *, out_dtype, tm, tn, tk=None, col_offset=0, n_cols=None,
            epilogue=None, name="matmul"):
    M = lhs_list[0].shape[0]
    K, N_total = w.shape
    N = N_total - col_offset if n_cols is None else n_cols
    k_sizes = tuple(a.shape[1] for a in lhs_list)
    assert sum(k_sizes) == K and M % tm == 0 and N % tn == 0 and col_offset % tn == 0
    joff = col_offset // tn
    if tk is None:
        n_k = 1
        in_specs = [pl.BlockSpec((tm, ks), lambda i, j: (i, 0)) for ks in k_sizes]
        in_specs.append(pl.BlockSpec((K, tn), lambda i, j: (0, j + joff)))
        out_specs = pl.BlockSpec((tm, tn), lambda i, j: (i, j))
        grid = (M // tm, N // tn)
        scratch = []
        sem = ("arbitrary", "arbitrary")
        kern_k_sizes = k_sizes
    else:
        assert len(lhs_list) == 1 and K % tk == 0
        n_k = K // tk
        in_specs = [pl.BlockSpec((tm, tk), lambda i, j, k: (i, k)),
                    pl.BlockSpec((tk, tn), lambda i, j, k: (k, j + joff))]
        out_specs = pl.BlockSpec((tm, tn), lambda i, j, k: (i, j))
        grid = (M // tm, N // tn, n_k)
        scratch = [pltpu.VMEM((tm, tn), F32)]
        sem = ("arbitrary", "arbitrary", "arbitrary")
        kern_k_sizes = (tk,)
    return pl.pallas_call(
        functools.partial(_matmul_kernel, n_lhs=len(lhs_list), k_sizes=kern_k_sizes,
                          n_k=n_k, epilogue=epilogue),
        grid=grid,
        in_specs=in_specs,
        out_specs=out_specs,
        out_shape=jax.ShapeDtypeStruct((M, N), out_dtype),
        scratch_shapes=scratch,
        compiler_params=_params(sem),
        name=name,
    )(*lhs_list, w)


def _bucket_thresholds():
    max_exact = N_BUCKETS // 2
    ths = []
    for b in range(max_exact + 1, N_BUCKETS):
        n = max_exact
        while True:
            v = max_exact + int(math.log(n / max_exact) / math.log(MAX_DISTANCE / max_exact)
                                * (N_BUCKETS - max_exact))
            if min(v, N_BUCKETS - 1) >= b:
                break
            n += 1
        ths.append(n)
    return ths


def _bias_kernel(rb_ref, o_ref):
    h = pl.program_id(0)
    t = pl.program_id(1)
    T = ATTN_BLOCK
    row = lax.broadcasted_iota(jnp.int32, (T, T), 0)
    col = lax.broadcasted_iota(jnp.int32, (T, T), 1)
    d = row - col + t * T
    n = jnp.maximum(d, 0)
    bucket = jnp.minimum(n, N_BUCKETS // 2)
    for th in _bucket_thresholds():
        bucket = bucket + (n >= th).astype(jnp.int32)
    far = rb_ref[N_BUCKETS - 1, h]
    tile = jnp.zeros((T, T), F32)
    for b in range(N_BUCKETS - 1):
        tile = jnp.where(bucket == b, rb_ref[b, h] - far, tile)
    o_ref[0, 0] = jnp.where(d >= 0, tile, -jnp.inf)


def _bias_tiles(rel_bias):
    T = ATTN_BLOCK
    return pl.pallas_call(
        _bias_kernel,
        grid=(N_HEADS, 2),
        in_specs=[pl.BlockSpec(memory_space=pltpu.SMEM)],
        out_specs=pl.BlockSpec((1, 1, T, T), lambda h, t: (h, t, 0, 0)),
        out_shape=jax.ShapeDtypeStruct((N_HEADS, 2, T, T), F32),
        compiler_params=_params(("arbitrary", "arbitrary")),
        name="bias_tiles",
    )(rel_bias)


def _attn_kernel(q_ref, k_ref, v_ref, bias_ref, lam_ref, g_ref, o_ref,
                 m_ref, l_ref, acc_ref, *, lambda_init):
    T = ATTN_BLOCK
    qi = pl.program_id(2)
    scale = HEAD_DIM ** -0.5

    m_ref[...] = jnp.full(m_ref.shape, -jnp.inf, F32)
    l_ref[...] = jnp.zeros(l_ref.shape, F32)
    acc_ref[...] = jnp.zeros(acc_ref.shape, F32)

    def step(j, bias):
        start = pl.multiple_of(j * T, T)
        k = k_ref[pl.ds(start, T), :]
        v = v_ref[pl.ds(start, T), :]
        for c in range(2):
            lo, hi = c * HEAD_DIM, (c + 1) * HEAD_DIM
            s = lax.dot_general(q_ref[:, lo:hi], k[:, lo:hi], (((1,), (1,)), ((), ())),
                                preferred_element_type=F32) * scale
            if bias is not None:
                s = s + bias
            m_old = m_ref[c]
            m_new = jnp.maximum(m_old, jnp.max(s, axis=-1, keepdims=True))
            alpha = jnp.exp(m_old - m_new)
            p = jnp.exp(s - m_new)
            l_ref[c] = alpha * l_ref[c] + jnp.sum(p, axis=-1, keepdims=True)
            acc_ref[c] = alpha * acc_ref[c] + jnp.dot(p.astype(BF16), v,
                                                      preferred_element_type=F32)
            m_ref[c] = m_new

    def plain(j, carry):
        step(j, None)
        return carry

    lax.fori_loop(0, jnp.maximum(qi - 1, 0), plain, 0)

    @pl.when(qi >= 1)
    def _():
        step(qi - 1, bias_ref[0, 1])

    step(qi, bias_ref[0, 0])

    lam_v = lam_ref[...]
    lam = (jnp.exp(jnp.sum(lam_v[0:1] * lam_v[1:2], axis=-1, keepdims=True))
           - jnp.exp(jnp.sum(lam_v[2:3] * lam_v[3:4], axis=-1, keepdims=True))
           + lambda_init)
    out = acc_ref[0] / l_ref[0] - lam * (acc_ref[1] / l_ref[1])
    o_ref[...] = (_rms(out, g_ref[...]) * (1.0 - lambda_init)).astype(o_ref.dtype)


def _attention(qkv, bias_tiles, lam_vecs, subln_g, lambda_init, B, S):
    T = ATTN_BLOCK
    nq = S // T
    return pl.pallas_call(
        functools.partial(_attn_kernel, lambda_init=lambda_init),
        grid=(B, N_HEADS, nq),
        in_specs=[
            pl.BlockSpec((T, V_DIM), lambda b, h, i: (b * nq + i, h)),
            pl.BlockSpec((S, V_DIM), lambda b, h, i: (b, N_HEADS + h)),
            pl.BlockSpec((S, V_DIM), lambda b, h, i: (b, 2 * N_HEADS + h)),
            pl.BlockSpec((1, 2, T, T), lambda b, h, i: (h, 0, 0, 0)),
            pl.BlockSpec((4, HEAD_DIM), lambda b, h, i: (0, 0)),
            pl.BlockSpec((1, V_DIM), lambda b, h, i: (0, 0)),
        ],
        out_specs=pl.BlockSpec((T, V_DIM), lambda b, h, i: (b * nq + i, h)),
        out_shape=jax.ShapeDtypeStruct((B * S, ATTN_WIDTH), BF16),
        scratch_shapes=[pltpu.VMEM((2, T, 1), F32), pltpu.VMEM((2, T, 1), F32),
                        pltpu.VMEM((2, T, V_DIM), F32)],
        compiler_params=_params(("arbitrary", "arbitrary", "arbitrary")),
        name="diff_attention",
    )(qkv, qkv, qkv, bias_tiles, lam_vecs, subln_g.reshape(1, V_DIM))


def _gate_kernel(z_ref, ws_ref, bs_ref, vg_ref, vb_ref, o_ref):
    z = z_ref[...]
    a = 0.5 * z * (1.0 + lax.erf(z * math.sqrt(0.5)))
    u = a[:, :GMLP_WIDTH]
    v = a[:, GMLP_WIDTH:]
    mu = jnp.mean(v, axis=-1, keepdims=True)
    vc = v - mu
    var = jnp.mean(vc * vc, axis=-1, keepdims=True)
    vn = (vc * lax.rsqrt(var + NORM_EPS) * vg_ref[...] + vb_ref[...]).astype(BF16)
    row = lax.broadcasted_iota(jnp.int32, (CHUNK, CHUNK), 0)
    col = lax.broadcasted_iota(jnp.int32, (CHUNK, CHUNK), 1)
    causal = row >= col
    bs = bs_ref[...]
    for g in range(N_GROUPS):
        ws = jnp.where(causal, ws_ref[g], 0.0).astype(BF16)
        cols = slice(g * GROUP_DIM, (g + 1) * GROUP_DIM)
        for n in range(z.shape[0] // CHUNK):
            rows = slice(n * CHUNK, (n + 1) * CHUNK)
            mixed = jnp.dot(ws, vn[rows, cols], preferred_element_type=F32) + bs[:, g:g + 1]
            o_ref[rows, cols] = (u[rows, cols] * mixed).astype(o_ref.dtype)


def _spatial_gate(z2, w_s, b_s, v_g, v_b):
    M = z2.shape[0]
    R = ROW_BLOCK
    return pl.pallas_call(
        _gate_kernel,
        grid=(M // R,),
        in_specs=[
            pl.BlockSpec((R, 2 * GMLP_WIDTH), lambda i: (i, 0)),
            pl.BlockSpec((N_GROUPS, CHUNK, CHUNK), lambda i: (0, 0, 0)),
            pl.BlockSpec((CHUNK, N_GROUPS), lambda i: (0, 0)),
            pl.BlockSpec((1, GMLP_WIDTH), lambda i: (0, 0)),
            pl.BlockSpec((1, GMLP_WIDTH), lambda i: (0, 0)),
        ],
        out_specs=pl.BlockSpec((R, GMLP_WIDTH), lambda i: (i, 0)),
        out_shape=jax.ShapeDtypeStruct((M, GMLP_WIDTH), BF16),
        compiler_params=_params(("arbitrary",)),
        name="spatial_gate",
    )(z2, w_s, b_s.T, v_g.reshape(1, GMLP_WIDTH), v_b.reshape(1, GMLP_WIDTH))


def kernel(x, c, rel_bias, w_ada, b_ada, pre_mix_g, w_in, lambda_q1, lambda_k1,
           lambda_q2, lambda_k2, subln_g, v_norm_g, v_norm_b, w_s, b_s, w_out,
           post_mix_g, pre_mlp_g, w_1, w_2, post_mlp_g):
    B, S, D = x.shape
    assert (D, x.dtype) == (D_MODEL, F32) and S % ATTN_BLOCK == 0 and S % ROW_BLOCK == 0
    M = B * S
    mod = _ada_mod(c, w_ada, b_ada)
    bias_tiles = _bias_tiles(rel_bias)
    x2 = x.reshape(M, D)
    h = _prenorm(x2, pre_mix_g[0], mod[0], 0, S)
    for l in range(DEPTH):
        lambda_init = 0.8 - 0.6 * math.exp(-0.3 * l)
        w_in_b = w_in[l].astype(BF16)
        qkv = _matmul([h], w_in_b, out_dtype=BF16, tm=1024, tn=1024, n_cols=QKV_COLS,
                      name="in_proj_qkv")
        z = _matmul([h], w_in_b, out_dtype=F32, tm=1024, tn=1024, col_offset=QKV_COLS,
                    name="in_proj_z")
        lam_vecs = jnp.stack([lambda_q1[l], lambda_k1[l], lambda_q2[l], lambda_k2[l]])
        attn = _attention(qkv, bias_tiles, lam_vecs, subln_g[l], lambda_init, B, S)
        gm = _spatial_gate(z, w_s[l], b_s[l], v_norm_g[l], v_norm_b[l])
        y = _matmul([attn, gm], w_out[l].astype(BF16), out_dtype=F32, tm=1024, tn=1024,
                    name="out_proj")
        mod_same = jnp.stack([mod[l], mod[l]], axis=1)
        x2, h = _postnorm(y, x2, post_mix_g[l], mod_same, pre_mlp_g[l], 2, 3, S)
        a = _matmul([h], w_1[l].astype(BF16), out_dtype=BF16, tm=1024, tn=1024,
                    epilogue="relu2", name="mlp_up")
        y = _matmul([a], w_2[l].astype(BF16), out_dtype=F32, tm=1024, tn=1024, tk=4096,
                    name="mlp_down")
        if l + 1 < DEPTH:
            mod_next = jnp.stack([mod[l], mod[l + 1]], axis=1)
            x2, h = _postnorm(y, x2, post_mlp_g[l], mod_next, pre_mix_g[l + 1], 5, 0, S)
        else:
            mod_same = jnp.stack([mod[l], mod[l]], axis=1)
            x2, _ = _postnorm(y, x2, post_mlp_g[l], mod_same, post_mlp_g[l], 5, None, S)
    return x2.reshape(B, S, D)
```

```python
import functools
import math

import jax
import jax.numpy as jnp
from jax import lax
from jax.experimental import pallas as pl
from jax.experimental.pallas import tpu as pltpu

F32 = jnp.float32
BF16 = jnp.bfloat16

D_MODEL = 4096
DEPTH = 2
ATTN_WIDTH = 2048
GMLP_WIDTH = 2048
HEAD_DIM = 128
V_DIM = 2 * HEAD_DIM
N_HEADS = ATTN_WIDTH // V_DIM
N_GROUPS = 8
GROUP_DIM = GMLP_WIDTH // N_GROUPS
CHUNK = 128
D_FF = 4 * D_MODEL
N_BUCKETS = 32
MAX_DISTANCE = 128
NORM_EPS = 1e-6
N_MOD = 6
QKV_COLS = 3 * ATTN_WIDTH
IN_COLS = QKV_COLS + 2 * GMLP_WIDTH

V7X_VMEM_BYTES = 64 * 1024 * 1024
VMEM_LIMIT_BYTES = V7X_VMEM_BYTES - 8 * 1024 * 1024
SUBLANES = 8
LANES = 128

ATTN_BLOCK = 512
ROW_BLOCK = 256


def _params(semantics):
    return pltpu.CompilerParams(dimension_semantics=semantics,
                                vmem_limit_bytes=VMEM_LIMIT_BYTES)


def _ada_kernel(c_ref, w_ref, b_ref, o_ref):
    c = c_ref[...]
    c_act = (c * jax.nn.sigmoid(c)).astype(BF16)
    w = w_ref[0].astype(BF16)
    o_ref[0] = jnp.dot(c_act, w, preferred_element_type=F32) + b_ref[0]


def _ada_mod(c, w_ada, b_ada):
    B, D = c.shape
    L, _, N = w_ada.shape
    tn = 512
    c_pad = jnp.zeros((SUBLANES, D), F32).at[:B].set(c)
    out = pl.pallas_call(
        _ada_kernel,
        grid=(L, N // tn),
        in_specs=[
            pl.BlockSpec((SUBLANES, D), lambda l, j: (0, 0)),
            pl.BlockSpec((1, D, tn), lambda l, j: (l, 0, j)),
            pl.BlockSpec((1, 1, tn), lambda l, j: (l, 0, j)),
        ],
        out_specs=pl.BlockSpec((1, SUBLANES, tn), lambda l, j: (l, 0, j)),
        out_shape=jax.ShapeDtypeStruct((L, SUBLANES, N), F32),
        compiler_params=_params(("arbitrary", "arbitrary")),
        name="ada_mod",
    )(c_pad, w_ada, b_ada.reshape(L, 1, N))
    return out[:, :B].reshape(L, B, N_MOD, D)


def _rms(x, g):
    ms = jnp.mean(x * x, axis=-1, keepdims=True)
    return x * lax.rsqrt(ms + NORM_EPS) * g


def _prenorm_kernel(x_ref, g_ref, mod_ref, h_ref, *, shift_idx):
    shift = mod_ref[0, shift_idx:shift_idx + 1, :]
    scale = mod_ref[0, shift_idx + 1:shift_idx + 2, :]
    h = _rms(x_ref[...], g_ref[...]) * (1.0 + scale) + shift
    h_ref[...] = h.astype(BF16)


def _prenorm(x2, g, mod, shift_idx, rows_per_batch):
    M, D = x2.shape
    tm = ROW_BLOCK
    bpb = rows_per_batch // tm
    return pl.pallas_call(
        functools.partial(_prenorm_kernel, shift_idx=shift_idx),
        grid=(M // tm,),
        in_specs=[
            pl.BlockSpec((tm, D), lambda i: (i, 0)),
            pl.BlockSpec((1, D), lambda i: (0, 0)),
            pl.BlockSpec((1, N_MOD, D), lambda i: (i // bpb, 0, 0)),
        ],
        out_specs=pl.BlockSpec((tm, D), lambda i: (i, 0)),
        out_shape=jax.ShapeDtypeStruct((M, D), BF16),
        compiler_params=_params(("arbitrary",)),
        name="prenorm",
    )(x2, g.reshape(1, D), mod)


def _postnorm_kernel(y_ref, x_ref, pg_ref, mod_ref, ng_ref, xo_ref, h_ref=None, *,
                     gate_idx, next_shift_idx):
    gate = mod_ref[0, gate_idx:gate_idx + 1, :]
    x_new = x_ref[...] + gate * _rms(y_ref[...], pg_ref[...])
    xo_ref[...] = x_new
    if next_shift_idx is not None:
        shift = mod_ref[1, next_shift_idx:next_shift_idx + 1, :]
        scale = mod_ref[1, next_shift_idx + 1:next_shift_idx + 2, :]
        h = _rms(x_new, ng_ref[...]) * (1.0 + scale) + shift
        h_ref[...] = h.astype(BF16)


def _postnorm(y2, x2, post_g, mod_pair, next_g, gate_idx, next_shift_idx, rows_per_batch):
    M, D = x2.shape
    tm = ROW_BLOCK
    bpb = rows_per_batch // tm
    row_spec = pl.BlockSpec((tm, D), lambda i: (i, 0))
    vec_spec = pl.BlockSpec((1, D), lambda i: (0, 0))
    with_next = next_shift_idx is not None
    kern = functools.partial(_postnorm_kernel, gate_idx=gate_idx,
                             next_shift_idx=next_shift_idx)
    out_shape = [jax.ShapeDtypeStruct((M, D), F32)]
    out_specs = [row_spec]
    if with_next:
        out_shape.append(jax.ShapeDtypeStruct((M, D), BF16))
        out_specs.append(row_spec)
    res = pl.pallas_call(
        kern,
        grid=(M // tm,),
        in_specs=[
            row_spec, row_spec, vec_spec,
            pl.BlockSpec((None, 2, N_MOD, D), lambda i: (i // bpb, 0, 0, 0)),
            vec_spec,
        ],
        out_specs=out_specs,
        out_shape=out_shape,
        compiler_params=_params(("arbitrary",)),
        name="postnorm",
    )(y2, x2, post_g.reshape(1, D), mod_pair, next_g.reshape(1, D))
    return (res[0], res[1]) if with_next else (res[0], None)


def _matmul_kernel(*refs, n_lhs, k_sizes, n_k, epilogue):
    lhs_refs = refs[:n_lhs]
    w_ref = refs[n_lhs]
    o_ref = refs[n_lhs + 1]
    acc = None
    off = 0
    for a_ref, ks in zip(lhs_refs, k_sizes):
        part = jnp.dot(a_ref[...], w_ref[off:off + ks, :], preferred_element_type=F32)
        acc = part if acc is None else acc + part
        off += ks

    def finish(r):
        if epilogue == "relu2":
            r = jnp.square(jnp.maximum(r, 0.0))
        o_ref[...] = r.astype(o_ref.dtype)

    if n_k == 1:
        finish(acc)
    else:
        acc_ref = refs[n_lhs + 2]
        k = pl.program_id(2)

        @pl.when(k == 0)
        def _():
            acc_ref[...] = acc

        @pl.when(k > 0)
        def _():
            acc_ref[...] += acc

        @pl.when(k == n_k - 1)
        def _():
            finish(acc_ref[...])


def _matmul(lhs_list, w, *, out_dtype, tm, tn, tk=None, col_offset=0, n_cols=None,
            epilogue=None, name="matmul"):
    M = lhs_list[0].shape[0]
    K, N_total = w.shape
    N = N_total - col_offset if n_cols is None else n_cols
    k_sizes = tuple(a.shape[1] for a in lhs_list)
    assert sum(k_sizes) == K and M % tm == 0 and N % tn == 0 and col_offset % tn == 0
    joff = col_offset // tn
    if tk is None:
        n_k = 1
        in_specs = [pl.BlockSpec((tm, ks), lambda i, j: (i, 0)) for ks in k_sizes]
        in_specs.append(pl.BlockSpec((K, tn), lambda i, j: (0, j + joff)))
        out_specs = pl.BlockSpec((tm, tn), lambda i, j: (i, j))
        grid = (M // tm, N // tn)
        scratch = []
        sem = ("arbitrary", "arbitrary")
        kern_k_sizes = k_sizes
    else:
        assert len(lhs_list) == 1 and K % tk == 0
        n_k = K // tk
        in_specs = [pl.BlockSpec((tm, tk), lambda i, j, k: (i, k)),
                    pl.BlockSpec((tk, tn), lambda i, j, k: (k, j + joff))]
        out_specs = pl.BlockSpec((tm, tn), lambda i, j, k: (i, j))
        grid = (M // tm, N // tn, n_k)
        scratch = [pltpu.VMEM((tm, tn), F32)]
        sem = ("arbitrary", "arbitrary", "arbitrary")
        kern_k_sizes = (tk,)
    return pl.pallas_call(
        functools.partial(_matmul_kernel, n_lhs=len(lhs_list), k_sizes=kern_k_sizes,
                          n_k=n_k, epilogue=epilogue),
        grid=grid,
        in_specs=in_specs,
        out_specs=out_specs,
        out_shape=jax.ShapeDtypeStruct((M, N), out_dtype),
        scratch_shapes=scratch,
        compiler_params=_params(sem),
        name=name,
    )(*lhs_list, w)


def _bucket_thresholds():
    max_exact = N_BUCKETS // 2
    ths = []
    for b in range(max_exact + 1, N_BUCKETS):
        n = max_exact
        while True:
            v = max_exact + int(math.log(n / max_exact) / math.log(MAX_DISTANCE / max_exact)
                                * (N_BUCKETS - max_exact))
            if min(v, N_BUCKETS - 1) >= b:
                break
            n += 1
        ths.append(n)
    return ths


def _bias_kernel(rb_ref, o_ref):
    h = pl.program_id(0)
    t = pl.program_id(1)
    T = ATTN_BLOCK
    row = lax.broadcasted_iota(jnp.int32, (T, T), 0)
    col = lax.broadcasted_iota(jnp.int32, (T, T), 1)
    d = row - col + t * T
    n = jnp.maximum(d, 0)
    bucket = jnp.minimum(n, N_BUCKETS // 2)
    for th in _bucket_thresholds():
        bucket = bucket + (n >= th).astype(jnp.int32)
    far = rb_ref[N_BUCKETS - 1, h]
    tile = jnp.zeros((T, T), F32)
    for b in range(N_BUCKETS - 1):
        tile = jnp.where(bucket == b, rb_ref[b, h] - far, tile)
    o_ref[0, 0] = jnp.where(d >= 0, tile, -jnp.inf)


def _bias_tiles(rel_bias):
    T = ATTN_BLOCK
    return pl.pallas_call(
        _bias_kernel,
        grid=(N_HEADS, 2),
        in_specs=[pl.BlockSpec(memory_space=pltpu.SMEM)],
        out_specs=pl.BlockSpec((1, 1, T, T), lambda h, t: (h, t, 0, 0)),
        out_shape=jax.ShapeDtypeStruct((N_HEADS, 2, T, T), F32),
        compiler_params=_params(("arbitrary", "arbitrary")),
        name="bias_tiles",
    )(rel_bias)


def _lane_tile(x, n):
    return x if n == 1 else jnp.concatenate([x] * n, axis=1)


def _attn_kernel(q_ref, k_ref, v_ref, bias_ref, lam_ref, g_ref, o_ref,
                 m_ref, l_ref, acc_ref, *, lambda_init):
    T = ATTN_BLOCK
    qi = pl.program_id(2)
    scale = HEAD_DIM ** -0.5

    m_ref[...] = jnp.full(m_ref.shape, -jnp.inf, F32)
    l_ref[...] = jnp.zeros(l_ref.shape, F32)
    acc_ref[...] = jnp.zeros(acc_ref.shape, F32)

    def step(j, bias):
        start = pl.multiple_of(j * T, T)
        k = k_ref[pl.ds(start, T), :]
        v = v_ref[pl.ds(start, T), :]
        for c in range(2):
            lo, hi = c * HEAD_DIM, (c + 1) * HEAD_DIM
            s = lax.dot_general(q_ref[:, lo:hi], k[:, lo:hi], (((1,), (1,)), ((), ())),
                                preferred_element_type=F32) * scale
            if bias is not None:
                s = s + bias
            m_old = m_ref[c]
            m_new = jnp.maximum(m_old, jnp.max(s, axis=-1, keepdims=True))
            alpha = jnp.exp(m_old - m_new)
            p = jnp.exp(s - _lane_tile(m_new, T // LANES))
            p_part = p[:, :LANES]
            for t in range(1, T // LANES):
                p_part = p_part + p[:, t * LANES:(t + 1) * LANES]
            l_ref[c] = alpha * l_ref[c] + p_part
            acc_ref[c] = (_lane_tile(alpha, V_DIM // LANES) * acc_ref[c]
                          + jnp.dot(p.astype(BF16), v, preferred_element_type=F32))
            m_ref[c] = m_new

    def plain(j, carry):
        step(j, None)
        return carry

    lax.fori_loop(0, jnp.maximum(qi - 1, 0), plain, 0)

    @pl.when(qi >= 1)
    def _():
        step(qi - 1, bias_ref[0, 1])

    step(qi, bias_ref[0, 0])

    lam_v = lam_ref[...]
    lam = (jnp.exp(jnp.sum(lam_v[0:1] * lam_v[1:2], axis=-1, keepdims=True))
           - jnp.exp(jnp.sum(lam_v[2:3] * lam_v[3:4], axis=-1, keepdims=True))
           + lambda_init)
    l0 = jnp.sum(l_ref[0], axis=-1, keepdims=True)
    l1 = jnp.sum(l_ref[1], axis=-1, keepdims=True)
    out = acc_ref[0] / l0 - lam * (acc_ref[1] / l1)
    o_ref[...] = (_rms(out, g_ref[...]) * (1.0 - lambda_init)).astype(o_ref.dtype)


def _attention(qkv, bias_tiles, lam_vecs, subln_g, lambda_init, B, S):
    T = ATTN_BLOCK
    nq = S // T
    return pl.pallas_call(
        functools.partial(_attn_kernel, lambda_init=lambda_init),
        grid=(B, N_HEADS, nq),
        in_specs=[
            pl.BlockSpec((T, V_DIM), lambda b, h, i: (b * nq + i, h)),
            pl.BlockSpec((S, V_DIM), lambda b, h, i: (b, N_HEADS + h)),
            pl.BlockSpec((S, V_DIM), lambda b, h, i: (b, 2 * N_HEADS + h)),
            pl.BlockSpec((1, 2, T, T), lambda b, h, i: (h, 0, 0, 0)),
            pl.BlockSpec((4, HEAD_DIM), lambda b, h, i: (0, 0)),
            pl.BlockSpec((1, V_DIM), lambda b, h, i: (0, 0)),
        ],
        out_specs=pl.BlockSpec((T, V_DIM), lambda b, h, i: (b * nq + i, h)),
        out_shape=jax.ShapeDtypeStruct((B * S, ATTN_WIDTH), BF16),
        scratch_shapes=[pltpu.VMEM((2, T, LANES), F32), pltpu.VMEM((2, T, LANES), F32),
                        pltpu.VMEM((2, T, V_DIM), F32)],
        compiler_params=_params(("arbitrary", "arbitrary", "arbitrary")),
        name="diff_attention",
    )(qkv, qkv, qkv, bias_tiles, lam_vecs, subln_g.reshape(1, V_DIM))


def _gate_kernel(z_ref, ws_ref, bs_ref, vg_ref, vb_ref, o_ref):
    z = z_ref[...]
    a = 0.5 * z * (1.0 + lax.erf(z * math.sqrt(0.5)))
    u = a[:, :GMLP_WIDTH]
    v = a[:, GMLP_WIDTH:]
    mu = jnp.mean(v, axis=-1, keepdims=True)
    vc = v - mu
    var = jnp.mean(vc * vc, axis=-1, keepdims=True)
    vn = (vc * lax.rsqrt(var + NORM_EPS) * vg_ref[...] + vb_ref[...]).astype(BF16)
    row = lax.broadcasted_iota(jnp.int32, (CHUNK, CHUNK), 0)
    col = lax.broadcasted_iota(jnp.int32, (CHUNK, CHUNK), 1)
    causal = row >= col
    bs = bs_ref[...]
    for g in range(N_GROUPS):
        ws = jnp.where(causal, ws_ref[g], 0.0).astype(BF16)
        cols = slice(g * GROUP_DIM, (g + 1) * GROUP_DIM)
        for n in range(z.shape[0] // CHUNK):
            rows = slice(n * CHUNK, (n + 1) * CHUNK)
            mixed = jnp.dot(ws, vn[rows, cols], preferred_element_type=F32) + bs[:, g:g + 1]
            o_ref[rows, cols] = (u[rows, cols] * mixed).astype(o_ref.dtype)


def _spatial_gate(z2, w_s, b_s, v_g, v_b):
    M = z2.shape[0]
    R = ROW_BLOCK
    return pl.pallas_call(
        _gate_kernel,
        grid=(M // R,),
        in_specs=[
            pl.BlockSpec((R, 2 * GMLP_WIDTH), lambda i: (i, 0)),
            pl.BlockSpec((N_GROUPS, CHUNK, CHUNK), lambda i: (0, 0, 0)),
            pl.BlockSpec((CHUNK, N_GROUPS), lambda i: (0, 0)),
            pl.BlockSpec((1, GMLP_WIDTH), lambda i: (0, 0)),
            pl.BlockSpec((1, GMLP_WIDTH), lambda i: (0, 0)),
        ],
        out_specs=pl.BlockSpec((R, GMLP_WIDTH), lambda i: (i, 0)),
        out_shape=jax.ShapeDtypeStruct((M, GMLP_WIDTH), BF16),
        compiler_params=_params(("arbitrary",)),
        name="spatial_gate",
    )(z2, w_s, b_s.T, v_g.reshape(1, GMLP_WIDTH), v_b.reshape(1, GMLP_WIDTH))


def kernel(x, c, rel_bias, w_ada, b_ada, pre_mix_g, w_in, lambda_q1, lambda_k1,
           lambda_q2, lambda_k2, subln_g, v_norm_g, v_norm_b, w_s, b_s, w_out,
           post_mix_g, pre_mlp_g, w_1, w_2, post_mlp_g):
    B, S, D = x.shape
    assert (D, x.dtype) == (D_MODEL, F32) and S % ATTN_BLOCK == 0 and S % ROW_BLOCK == 0
    M = B * S
    mod = _ada_mod(c, w_ada, b_ada)
    bias_tiles = _bias_tiles(rel_bias)
    x2 = x.reshape(M, D)
    h = _prenorm(x2, pre_mix_g[0], mod[0], 0, S)
    for l in range(DEPTH):
        lambda_init = 0.8 - 0.6 * math.exp(-0.3 * l)
        w_in_b = w_in[l].astype(BF16)
        qkv = _matmul([h], w_in_b, out_dtype=BF16, tm=1024, tn=1024, n_cols=QKV_COLS,
                      name="in_proj_qkv")
        z = _matmul([h], w_in_b, out_dtype=F32, tm=1024, tn=1024, col_offset=QKV_COLS,
                    name="in_proj_z")
        lam_vecs = jnp.stack([lambda_q1[l], lambda_k1[l], lambda_q2[l], lambda_k2[l]])
        attn = _attention(qkv, bias_tiles, lam_vecs, subln_g[l], lambda_init, B, S)
        gm = _spatial_gate(z, w_s[l], b_s[l], v_norm_g[l], v_norm_b[l])
        y = _matmul([attn, gm], w_out[l].astype(BF16), out_dtype=F32, tm=1024, tn=1024,
                    name="out_proj")
        mod_same = jnp.stack([mod[l], mod[l]], axis=1)
        x2, h = _postnorm(y, x2, post_mix_g[l], mod_same, pre_mlp_g[l], 2, 3, S)
        a = _matmul([h], w_1[l].astype(BF16), out_dtype=BF16, tm=1024, tn=1024,
                    epilogue="relu2", name="mlp_up")
        y = _matmul([a], w_2[l].astype(BF16), out_dtype=F32, tm=1024, tn=1024, tk=4096,
                    name="mlp_down")
        if l + 1 < DEPTH:
            mod_next = jnp.stack([mod[l], mod[l + 1]], axis=1)
            x2, h = _postnorm(y, x2, post_mlp_g[l], mod_next, pre_mix_g[l + 1], 5, 0, S)
        else:
            mod_same = jnp.stack([mod[l], mod[l]], axis=1)
            x2, _ = _postnorm(y, x2, post_mlp_g[l], mod_same, post_mlp_g[l], 5, None, S)
    return x2.reshape(B, S, D)
```

```python
import functools
import math

import jax
import jax.numpy as jnp
from jax import lax
from jax.experimental import pallas as pl
from jax.experimental.pallas import tpu as pltpu

F32 = jnp.float32
BF16 = jnp.bfloat16

D_MODEL = 4096
DEPTH = 2
ATTN_WIDTH = 2048
GMLP_WIDTH = 2048
HEAD_DIM = 128
V_DIM = 2 * HEAD_DIM
N_HEADS = ATTN_WIDTH // V_DIM
N_GROUPS = 8
GROUP_DIM = GMLP_WIDTH // N_GROUPS
CHUNK = 128
D_FF = 4 * D_MODEL
N_BUCKETS = 32
MAX_DISTANCE = 128
NORM_EPS = 1e-6
N_MOD = 6
QKV_COLS = 3 * ATTN_WIDTH
IN_COLS = QKV_COLS + 2 * GMLP_WIDTH

V7X_VMEM_BYTES = 64 * 1024 * 1024
VMEM_LIMIT_BYTES = V7X_VMEM_BYTES - 8 * 1024 * 1024
SUBLANES = 8
BF16_SUBLANES = 16
LANES = 128
LOG2E = math.log2(math.e)

ATTN_BLOCK = 512
ROW_BLOCK = 256


def _params(semantics):
    return pltpu.CompilerParams(dimension_semantics=semantics,
                                vmem_limit_bytes=VMEM_LIMIT_BYTES)


def _ada_kernel(c_ref, w_ref, b_ref, o_ref):
    c = c_ref[...]
    c_act = (c * jax.nn.sigmoid(c)).astype(BF16)
    w = w_ref[0].astype(BF16)
    o_ref[0] = jnp.dot(c_act, w, preferred_element_type=F32) + b_ref[0]


def _ada_mod(c, w_ada, b_ada):
    B, D = c.shape
    L, _, N = w_ada.shape
    tn = 512
    c_pad = jnp.zeros((SUBLANES, D), F32).at[:B].set(c)
    out = pl.pallas_call(
        _ada_kernel,
        grid=(L, N // tn),
        in_specs=[
            pl.BlockSpec((SUBLANES, D), lambda l, j: (0, 0)),
            pl.BlockSpec((1, D, tn), lambda l, j: (l, 0, j)),
            pl.BlockSpec((1, 1, tn), lambda l, j: (l, 0, j)),
        ],
        out_specs=pl.BlockSpec((1, SUBLANES, tn), lambda l, j: (l, 0, j)),
        out_shape=jax.ShapeDtypeStruct((L, SUBLANES, N), F32),
        compiler_params=_params(("arbitrary", "arbitrary")),
        name="ada_mod",
    )(c_pad, w_ada, b_ada.reshape(L, 1, N))
    return out[:, :B].reshape(L, B, N_MOD, D)


def _cast_slab_spec(src, grid):
    n_steps = math.prod(grid)
    rows, cols = src.shape
    assert rows % (n_steps * BF16_SUBLANES) == 0
    strides = [math.prod(grid[d + 1:]) for d in range(len(grid))]

    def index_map(*g):
        return (sum(gi * st for gi, st in zip(g, strides)), 0)

    return pl.BlockSpec((rows // n_steps, cols), index_map)


def _rms(x, g):
    ms = jnp.mean(x * x, axis=-1, keepdims=True)
    return x * lax.rsqrt(ms + NORM_EPS) * g


def _prenorm_kernel(x_ref, g_ref, mod_ref, cast_in_ref, h_ref, cast_out_ref, *, shift_idx):
    cast_out_ref[...] = cast_in_ref[...].astype(BF16)
    shift = mod_ref[0, shift_idx:shift_idx + 1, :]
    scale = mod_ref[0, shift_idx + 1:shift_idx + 2, :]
    h = _rms(x_ref[...], g_ref[...]) * (1.0 + scale) + shift
    h_ref[...] = h.astype(BF16)


def _prenorm(x2, g, mod, shift_idx, rows_per_batch, cast_src):
    M, D = x2.shape
    tm = ROW_BLOCK
    bpb = rows_per_batch // tm
    grid = (M // tm,)
    cast_spec = _cast_slab_spec(cast_src, grid)
    return pl.pallas_call(
        functools.partial(_prenorm_kernel, shift_idx=shift_idx),
        grid=grid,
        in_specs=[
            pl.BlockSpec((tm, D), lambda i: (i, 0)),
            pl.BlockSpec((1, D), lambda i: (0, 0)),
            pl.BlockSpec((1, N_MOD, D), lambda i: (i // bpb, 0, 0)),
            cast_spec,
        ],
        out_specs=[pl.BlockSpec((tm, D), lambda i: (i, 0)), cast_spec],
        out_shape=[jax.ShapeDtypeStruct((M, D), BF16),
                   jax.ShapeDtypeStruct(cast_src.shape, BF16)],
        compiler_params=_params(("arbitrary",)),
        name="prenorm",
    )(x2, g.reshape(1, D), mod, cast_src)


def _postnorm_kernel(y_ref, x_ref, pg_ref, mod_ref, ng_ref, xo_ref, h_ref=None, *,
                     gate_idx, next_shift_idx):
    gate = mod_ref[0, gate_idx:gate_idx + 1, :]
    x_new = x_ref[...] + gate * _rms(y_ref[...], pg_ref[...])
    xo_ref[...] = x_new
    if next_shift_idx is not None:
        shift = mod_ref[1, next_shift_idx:next_shift_idx + 1, :]
        scale = mod_ref[1, next_shift_idx + 1:next_shift_idx + 2, :]
        h = _rms(x_new, ng_ref[...]) * (1.0 + scale) + shift
        h_ref[...] = h.astype(BF16)


def _postnorm(y2, x2, post_g, mod_pair, next_g, gate_idx, next_shift_idx, rows_per_batch):
    M, D = x2.shape
    tm = ROW_BLOCK
    bpb = rows_per_batch // tm
    row_spec = pl.BlockSpec((tm, D), lambda i: (i, 0))
    vec_spec = pl.BlockSpec((1, D), lambda i: (0, 0))
    with_next = next_shift_idx is not None
    kern = functools.partial(_postnorm_kernel, gate_idx=gate_idx,
                             next_shift_idx=next_shift_idx)
    out_shape = [jax.ShapeDtypeStruct((M, D), F32)]
    out_specs = [row_spec]
    if with_next:
        out_shape.append(jax.ShapeDtypeStruct((M, D), BF16))
        out_specs.append(row_spec)
    res = pl.pallas_call(
        kern,
        grid=(M // tm,),
        in_specs=[
            row_spec, row_spec, vec_spec,
            pl.BlockSpec((None, 2, N_MOD, D), lambda i: (i // bpb, 0, 0, 0)),
            vec_spec,
        ],
        out_specs=out_specs,
        out_shape=out_shape,
        compiler_params=_params(("arbitrary",)),
        name="postnorm",
    )(y2, x2, post_g.reshape(1, D), mod_pair, next_g.reshape(1, D))
    return (res[0], res[1]) if with_next else (res[0], None)


def _matmul_kernel(*refs, n_lhs, k_sizes, n_k, epilogue, n_cast):
    lhs_refs = refs[:n_lhs]
    w_ref = refs[n_lhs]
    n_in = n_lhs + 1 + n_cast
    o_ref = refs[n_in]
    for t in range(n_cast):
        refs[n_in + 1 + t][...] = refs[n_lhs + 1 + t][...].astype(BF16)
    acc = None
    off = 0
    for a_ref, ks in zip(lhs_refs, k_sizes):
        part = jnp.dot(a_ref[...], w_ref[off:off + ks, :], preferred_element_type=F32)
        acc = part if acc is None else acc + part
        off += ks

    def finish(r):
        if epilogue == "relu2":
            r = jnp.square(jnp.maximum(r, 0.0))
        o_ref[...] = r.astype(o_ref.dtype)

    if n_k == 1:
        finish(acc)
    else:
        acc_ref = refs[-1]
        k = pl.program_id(2)

        @pl.when(k == 0)
        def _():
            acc_ref[...] = acc

        @pl.when(k > 0)
        def _():
            acc_ref[...] += acc

        @pl.when(k == n_k - 1)
        def _():
            finish(acc_ref[...])


def _matmul(lhs_list, w, *, out_dtype, tm, tn, tk=None, col_offset=0, n_cols=None,
            epilogue=None, cast_srcs=(), name="matmul"):
    M = lhs_list[0].shape[0]
    K, N_total = w.shape
    N = N_total - col_offset if n_cols is None else n_cols
    k_sizes = tuple(a.shape[1] for a in lhs_list)
    assert sum(k_sizes) == K and M % tm == 0 and N % tn == 0 and col_offset % tn == 0
    joff = col_offset // tn
    if tk is None:
        n_k = 1
        in_specs = [pl.BlockSpec((tm, ks), lambda i, j: (i, 0)) for ks in k_sizes]
        in_specs.append(pl.BlockSpec((K, tn), lambda i, j: (0, j + joff)))
        out_specs = [pl.BlockSpec((tm, tn), lambda i, j: (i, j))]
        grid = (M // tm, N // tn)
        scratch = []
        kern_k_sizes = k_sizes
    else:
        assert len(lhs_list) == 1 and K % tk == 0
        n_k = K // tk
        in_specs = [pl.BlockSpec((tm, tk), lambda i, j, k: (i, k)),
                    pl.BlockSpec((tk, tn), lambda i, j, k: (k, j + joff))]
        out_specs = [pl.BlockSpec((tm, tn), lambda i, j, k: (i, j))]
        grid = (M // tm, N // tn, n_k)
        scratch = [pltpu.VMEM((tm, tn), F32)]
        kern_k_sizes = (tk,)
    out_shape = [jax.ShapeDtypeStruct((M, N), out_dtype)]
    operands = list(lhs_list) + [w]
    for src in cast_srcs:
        spec = _cast_slab_spec(src, grid)
        in_specs.append(spec)
        out_specs.append(spec)
        out_shape.append(jax.ShapeDtypeStruct(src.shape, BF16))
        operands.append(src)
    res = pl.pallas_call(
        functools.partial(_matmul_kernel, n_lhs=len(lhs_list), k_sizes=kern_k_sizes,
                          n_k=n_k, epilogue=epilogue, n_cast=len(cast_srcs)),
        grid=grid,
        in_specs=in_specs,
        out_specs=out_specs,
        out_shape=out_shape,
        scratch_shapes=scratch,
        compiler_params=_params(("arbitrary",) * len(grid)),
        name=name,
    )(*operands)
    return res if cast_srcs else res[0]


def _bucket_thresholds():
    max_exact = N_BUCKETS // 2
    ths = []
    for b in range(max_exact + 1, N_BUCKETS):
        n = max_exact
        while True:
            v = max_exact + int(math.log(n / max_exact) / math.log(MAX_DISTANCE / max_exact)
                                * (N_BUCKETS - max_exact))
            if min(v, N_BUCKETS - 1) >= b:
                break
            n += 1
        ths.append(n)
    return ths


def _bias_kernel(rb_ref, o_ref):
    h = pl.program_id(0)
    T, C = ATTN_BLOCK, MAX_DISTANCE
    row = lax.broadcasted_iota(jnp.int32, (C, C), 0)
    col = lax.broadcasted_iota(jnp.int32, (C, C), 1)
    far = rb_ref[N_BUCKETS - 1, h]

    def band_tile(offset):
        d = row - col + offset
        n = jnp.maximum(d, 0)
        bucket = jnp.minimum(n, N_BUCKETS // 2)
        for th in _bucket_thresholds():
            bucket = bucket + (n >= th).astype(jnp.int32)
        tile = jnp.zeros((C, C), F32)
        for b in range(N_BUCKETS - 1):
            tile = jnp.where(bucket == b, (rb_ref[b, h] - far) * LOG2E, tile)
        return jnp.where(d >= 0, tile, -jnp.inf)

    on_diag = band_tile(0)
    below_diag = band_tile(C)
    zeros = jnp.zeros((C, C), F32)
    future = jnp.full((C, C), -jnp.inf, F32)
    nb = T // C
    for i in range(nb):
        for j in range(nb):
            sub = on_diag if i == j else below_diag if i == j + 1 else zeros if i > j else future
            o_ref[0, 0, i * C:(i + 1) * C, j * C:(j + 1) * C] = sub
            near = i == 0 and j == nb - 1
            o_ref[0, 1, i * C:(i + 1) * C, j * C:(j + 1) * C] = below_diag if near else zeros


def _bias_tiles(rel_bias):
    T = ATTN_BLOCK
    return pl.pallas_call(
        _bias_kernel,
        grid=(N_HEADS,),
        in_specs=[pl.BlockSpec(memory_space=pltpu.SMEM)],
        out_specs=pl.BlockSpec((1, 2, T, T), lambda h: (h, 0, 0, 0)),
        out_shape=jax.ShapeDtypeStruct((N_HEADS, 2, T, T), F32),
        compiler_params=_params(("arbitrary",)),
        name="bias_tiles",
    )(rel_bias)


def _lane_tile(x, n):
    return x if n == 1 else jnp.concatenate([x] * n, axis=1)


def _attn_kernel(q_ref, k_ref, v_ref, bias_ref, lam_ref, g_ref, cast_in_ref, o_ref,
                 cast_out_ref, m_ref, l_ref, acc_ref, *, lambda_init):
    T = ATTN_BLOCK
    qi = pl.program_id(2)
    scale2 = HEAD_DIM ** -0.5 * LOG2E

    cast_out_ref[...] = cast_in_ref[...].astype(BF16)
    m_ref[...] = jnp.full(m_ref.shape, -jnp.inf, F32)
    l_ref[...] = jnp.zeros(l_ref.shape, F32)
    acc_ref[...] = jnp.zeros(acc_ref.shape, F32)

    def step(j, bias):
        start = pl.multiple_of(j * T, T)
        k = k_ref[pl.ds(start, T), :]
        v = v_ref[pl.ds(start, T), :]
        for c in range(2):
            lo, hi = c * HEAD_DIM, (c + 1) * HEAD_DIM
            s = lax.dot_general(q_ref[:, lo:hi], k[:, lo:hi], (((1,), (1,)), ((), ())),
                                preferred_element_type=F32) * scale2
            if bias is not None:
                s = s + bias
            m_old = m_ref[c]
            m_new = jnp.maximum(m_old, jnp.max(s, axis=-1, keepdims=True))
            alpha = jnp.exp2(m_old - m_new)
            p = jnp.exp2(s - _lane_tile(m_new, T // LANES))
            p_part = p[:, :LANES]
            for t in range(1, T // LANES):
                p_part = p_part + p[:, t * LANES:(t + 1) * LANES]
            l_ref[c] = alpha * l_ref[c] + p_part
            acc_ref[c] = (_lane_tile(alpha, V_DIM // LANES) * acc_ref[c]
                          + jnp.dot(p.astype(BF16), v, preferred_element_type=F32))
            m_ref[c] = m_new

    def plain(j, carry):
        step(j, None)
        return carry

    lax.fori_loop(0, jnp.maximum(qi - 1, 0), plain, 0)

    @pl.when(qi >= 1)
    def _():
        step(qi - 1, bias_ref[0, 1])

    step(qi, bias_ref[0, 0])

    lam_v = lam_ref[...]
    lam = (jnp.exp(jnp.sum(lam_v[0:1] * lam_v[1:2], axis=-1, keepdims=True))
           - jnp.exp(jnp.sum(lam_v[2:3] * lam_v[3:4], axis=-1, keepdims=True))
           + lambda_init)
    l0 = jnp.sum(l_ref[0], axis=-1, keepdims=True)
    l1 = jnp.sum(l_ref[1], axis=-1, keepdims=True)
    out = acc_ref[0] / l0 - lam * (acc_ref[1] / l1)
    o_ref[...] = (_rms(out, g_ref[...]) * (1.0 - lambda_init)).astype(o_ref.dtype)


def _attention(qkv, bias_tiles, lam_vecs, subln_g, lambda_init, B, S, cast_src):
    T = ATTN_BLOCK
    nq = S // T
    grid = (B, N_HEADS, nq)
    cast_spec = _cast_slab_spec(cast_src, grid)
    return pl.pallas_call(
        functools.partial(_attn_kernel, lambda_init=lambda_init),
        grid=grid,
        in_specs=[
            pl.BlockSpec((T, V_DIM), lambda b, h, i: (b * nq + i, h)),
            pl.BlockSpec((S, V_DIM), lambda b, h, i: (b, N_HEADS + h)),
            pl.BlockSpec((S, V_DIM), lambda b, h, i: (b, 2 * N_HEADS + h)),
            pl.BlockSpec((1, 2, T, T), lambda b, h, i: (h, 0, 0, 0)),
            pl.BlockSpec((4, HEAD_DIM), lambda b, h, i: (0, 0)),
            pl.BlockSpec((1, V_DIM), lambda b, h, i: (0, 0)),
            cast_spec,
        ],
        out_specs=[pl.BlockSpec((T, V_DIM), lambda b, h, i: (b * nq + i, h)), cast_spec],
        out_shape=[jax.ShapeDtypeStruct((B * S, ATTN_WIDTH), BF16),
                   jax.ShapeDtypeStruct(cast_src.shape, BF16)],
        scratch_shapes=[pltpu.VMEM((2, T, LANES), F32), pltpu.VMEM((2, T, LANES), F32),
                        pltpu.VMEM((2, T, V_DIM), F32)],
        compiler_params=_params(("arbitrary", "arbitrary", "arbitrary")),
        name="diff_attention",
    )(qkv, qkv, qkv, bias_tiles, lam_vecs, subln_g.reshape(1, V_DIM), cast_src)


def _gate_kernel(z_ref, ws_ref, bs_ref, vg_ref, vb_ref, o_ref):
    z = z_ref[...]
    a = 0.5 * z * (1.0 + lax.erf(z * math.sqrt(0.5)))
    u = a[:, :GMLP_WIDTH]
    v = a[:, GMLP_WIDTH:]
    mu = jnp.mean(v, axis=-1, keepdims=True)
    vc = v - mu
    var = jnp.mean(vc * vc, axis=-1, keepdims=True)
    vn = (vc * lax.rsqrt(var + NORM_EPS) * vg_ref[...] + vb_ref[...]).astype(BF16)
    row = lax.broadcasted_iota(jnp.int32, (CHUNK, CHUNK), 0)
    col = lax.broadcasted_iota(jnp.int32, (CHUNK, CHUNK), 1)
    causal = row >= col
    bs = bs_ref[...]
    for g in range(N_GROUPS):
        ws = jnp.where(causal, ws_ref[g], 0.0).astype(BF16)
        cols = slice(g * GROUP_DIM, (g + 1) * GROUP_DIM)
        for n in range(z.shape[0] // CHUNK):
            rows = slice(n * CHUNK, (n + 1) * CHUNK)
            mixed = jnp.dot(ws, vn[rows, cols], preferred_element_type=F32) + bs[:, g:g + 1]
            o_ref[rows, cols] = (u[rows, cols] * mixed).astype(o_ref.dtype)


def _spatial_gate(z2, w_s, b_s, v_g, v_b):
    M = z2.shape[0]
    R = ROW_BLOCK
    return pl.pallas_call(
        _gate_kernel,
        grid=(M // R,),
        in_specs=[
            pl.BlockSpec((R, 2 * GMLP_WIDTH), lambda i: (i, 0)),
            pl.BlockSpec((N_GROUPS, CHUNK, CHUNK), lambda i: (0, 0, 0)),
            pl.BlockSpec((CHUNK, N_GROUPS), lambda i: (0, 0)),
            pl.BlockSpec((1, GMLP_WIDTH), lambda i: (0, 0)),
            pl.BlockSpec((1, GMLP_WIDTH), lambda i: (0, 0)),
        ],
        out_specs=pl.BlockSpec((R, GMLP_WIDTH), lambda i: (i, 0)),
        out_shape=jax.ShapeDtypeStruct((M, GMLP_WIDTH), BF16),
        compiler_params=_params(("arbitrary",)),
        name="spatial_gate",
    )(z2, w_s, b_s.T, v_g.reshape(1, GMLP_WIDTH), v_b.reshape(1, GMLP_WIDTH))


def kernel(x, c, rel_bias, w_ada, b_ada, pre_mix_g, w_in, lambda_q1, lambda_k1,
           lambda_q2, lambda_k2, subln_g, v_norm_g, v_norm_b, w_s, b_s, w_out,
           post_mix_g, pre_mlp_g, w_1, w_2, post_mlp_g):
    B, S, D = x.shape
    assert (D, x.dtype) == (D_MODEL, F32) and S % ATTN_BLOCK == 0 and S % ROW_BLOCK == 0
    M = B * S
    mod = _ada_mod(c, w_ada, b_ada)
    bias_tiles = _bias_tiles(rel_bias)
    x2 = x.reshape(M, D)
    h, w_in_b = _prenorm(x2, pre_mix_g[0], mod[0], 0, S, cast_src=w_in[0])
    for l in range(DEPTH):
        lambda_init = 0.8 - 0.6 * math.exp(-0.3 * l)
        qkv = _matmul([h], w_in_b, out_dtype=BF16, tm=1024, tn=1024, n_cols=QKV_COLS,
                      name="in_proj_qkv")
        z, w_out_b = _matmul([h], w_in_b, out_dtype=F32, tm=1024, tn=1024,
                             col_offset=QKV_COLS, cast_srcs=(w_out[l],), name="in_proj_z")
        lam_vecs = jnp.stack([lambda_q1[l], lambda_k1[l], lambda_q2[l], lambda_k2[l]])
        attn, w_1_b = _attention(qkv, bias_tiles, lam_vecs, subln_g[l], lambda_init, B, S,
                                 cast_src=w_1[l])
        gm = _spatial_gate(z, w_s[l], b_s[l], v_norm_g[l], v_norm_b[l])
        y = _matmul([attn, gm], w_out_b, out_dtype=F32, tm=1024, tn=1024, name="out_proj")
        mod_same = jnp.stack([mod[l], mod[l]], axis=1)
        x2, h = _postnorm(y, x2, post_mix_g[l], mod_same, pre_mlp_g[l], 2, 3, S)
        next_w_in = (w_in[l + 1],) if l + 1 < DEPTH else ()
        a, w_2_b, *w_in_next_b = _matmul([h], w_1_b, out_dtype=BF16, tm=1024, tn=1024,
                                         epilogue="relu2", cast_srcs=(w_2[l],) + next_w_in,
                                         name="mlp_up")
        y = _matmul([a], w_2_b, out_dtype=F32, tm=1024, tn=1024, tk=4096, name="mlp_down")
        if l + 1 < DEPTH:
            w_in_b = w_in_next_b[0]
            mod_next = jnp.stack([mod[l], mod[l + 1]], axis=1)
            x2, h = _postnorm(y, x2, post_mlp_g[l], mod_next, pre_mix_g[l + 1], 5, 0, S)
        else:
            mod_same = jnp.stack([mod[l], mod[l]], axis=1)
            x2, _ = _postnorm(y, x2, post_mlp_g[l], mod_same, post_mlp_g[l], 5, None, S)
    return x2.reshape(B, S, D)
```

```python
import functools
import math

import jax
import jax.numpy as jnp
from jax import lax
from jax.experimental import pallas as pl
from jax.experimental.pallas import tpu as pltpu

F32 = jnp.float32
BF16 = jnp.bfloat16

D_MODEL = 4096
DEPTH = 2
ATTN_WIDTH = 2048
GMLP_WIDTH = 2048
HEAD_DIM = 128
V_DIM = 2 * HEAD_DIM
N_HEADS = ATTN_WIDTH // V_DIM
N_GROUPS = 8
GROUP_DIM = GMLP_WIDTH // N_GROUPS
CHUNK = 128
D_FF = 4 * D_MODEL
N_BUCKETS = 32
MAX_DISTANCE = 128
NORM_EPS = 1e-6
N_MOD = 6
QKV_COLS = 3 * ATTN_WIDTH
IN_COLS = QKV_COLS + 2 * GMLP_WIDTH

V7X_VMEM_BYTES = 64 * 1024 * 1024
VMEM_LIMIT_BYTES = V7X_VMEM_BYTES - 8 * 1024 * 1024
SUBLANES = 8
BF16_SUBLANES = 16
LANES = 128
LOG2E = math.log2(math.e)

ATTN_BLOCK = 512
ROW_BLOCK = 256


def _params(semantics):
    return pltpu.CompilerParams(dimension_semantics=semantics,
                                vmem_limit_bytes=VMEM_LIMIT_BYTES)


def _ada_kernel(c_ref, w_ref, b_ref, o_ref):
    c = c_ref[...]
    c_act = (c * jax.nn.sigmoid(c)).astype(BF16)
    w = w_ref[0].astype(BF16)
    o_ref[0] = jnp.dot(c_act, w, preferred_element_type=F32) + b_ref[0]


def _ada_mod(c, w_ada, b_ada):
    B, D = c.shape
    L, _, N = w_ada.shape
    tn = 512
    c_pad = jnp.zeros((SUBLANES, D), F32).at[:B].set(c)
    out = pl.pallas_call(
        _ada_kernel,
        grid=(L, N // tn),
        in_specs=[
            pl.BlockSpec((SUBLANES, D), lambda l, j: (0, 0)),
            pl.BlockSpec((1, D, tn), lambda l, j: (l, 0, j)),
            pl.BlockSpec((1, 1, tn), lambda l, j: (l, 0, j)),
        ],
        out_specs=pl.BlockSpec((1, SUBLANES, tn), lambda l, j: (l, 0, j)),
        out_shape=jax.ShapeDtypeStruct((L, SUBLANES, N), F32),
        compiler_params=_params(("arbitrary", "arbitrary")),
        name="ada_mod",
    )(c_pad, w_ada, b_ada.reshape(L, 1, N))
    return out[:, :B].reshape(L, B, N_MOD, D)


def _cast_slab_specs(cast_src, grid):
    stacked, layer = cast_src
    n_steps = math.prod(grid)
    _, rows, cols = stacked.shape
    assert rows % (n_steps * BF16_SUBLANES) == 0
    slab = rows // n_steps
    strides = [math.prod(grid[d + 1:]) for d in range(len(grid))]

    def step(*g):
        return sum(gi * st for gi, st in zip(g, strides))

    return (pl.BlockSpec((None, slab, cols), lambda *g: (layer, step(*g), 0)),
            pl.BlockSpec((slab, cols), lambda *g: (step(*g), 0)),
            jax.ShapeDtypeStruct((rows, cols), BF16))


def _rms(x, g):
    ms = jnp.mean(x * x, axis=-1, keepdims=True)
    return x * lax.rsqrt(ms + NORM_EPS) * g


def _prenorm_kernel(x_ref, g_ref, mod_ref, cast_in_ref, h_ref, cast_out_ref, *, shift_idx):
    cast_out_ref[...] = cast_in_ref[...].astype(BF16)
    shift = mod_ref[0, shift_idx:shift_idx + 1, :]
    scale = mod_ref[0, shift_idx + 1:shift_idx + 2, :]
    h = _rms(x_ref[...], g_ref[...]) * (1.0 + scale) + shift
    h_ref[...] = h.astype(BF16)


def _prenorm(x2, g, mod, shift_idx, rows_per_batch, cast_src):
    M, D = x2.shape
    tm = ROW_BLOCK
    bpb = rows_per_batch // tm
    grid = (M // tm,)
    cast_in_spec, cast_out_spec, cast_shape = _cast_slab_specs(cast_src, grid)
    return pl.pallas_call(
        functools.partial(_prenorm_kernel, shift_idx=shift_idx),
        grid=grid,
        in_specs=[
            pl.BlockSpec((tm, D), lambda i: (i, 0)),
            pl.BlockSpec((1, D), lambda i: (0, 0)),
            pl.BlockSpec((1, N_MOD, D), lambda i: (i // bpb, 0, 0)),
            cast_in_spec,
        ],
        out_specs=[pl.BlockSpec((tm, D), lambda i: (i, 0)), cast_out_spec],
        out_shape=[jax.ShapeDtypeStruct((M, D), BF16), cast_shape],
        compiler_params=_params(("arbitrary",)),
        name="prenorm",
    )(x2, g.reshape(1, D), mod, cast_src[0])


def _postnorm_kernel(y_ref, x_ref, pg_ref, mod_ref, ng_ref, xo_ref, h_ref=None, *,
                     gate_idx, next_shift_idx):
    gate = mod_ref[0, gate_idx:gate_idx + 1, :]
    x_new = x_ref[...] + gate * _rms(y_ref[...], pg_ref[...])
    xo_ref[...] = x_new
    if next_shift_idx is not None:
        shift = mod_ref[1, next_shift_idx:next_shift_idx + 1, :]
        scale = mod_ref[1, next_shift_idx + 1:next_shift_idx + 2, :]
        h = _rms(x_new, ng_ref[...]) * (1.0 + scale) + shift
        h_ref[...] = h.astype(BF16)


def _postnorm(y2, x2, post_g, mod_pair, next_g, gate_idx, next_shift_idx, rows_per_batch):
    M, D = x2.shape
    tm = ROW_BLOCK
    bpb = rows_per_batch // tm
    row_spec = pl.BlockSpec((tm, D), lambda i: (i, 0))
    vec_spec = pl.BlockSpec((1, D), lambda i: (0, 0))
    with_next = next_shift_idx is not None
    kern = functools.partial(_postnorm_kernel, gate_idx=gate_idx,
                             next_shift_idx=next_shift_idx)
    out_shape = [jax.ShapeDtypeStruct((M, D), F32)]
    out_specs = [row_spec]
    if with_next:
        out_shape.append(jax.ShapeDtypeStruct((M, D), BF16))
        out_specs.append(row_spec)
    res = pl.pallas_call(
        kern,
        grid=(M // tm,),
        in_specs=[
            row_spec, row_spec, vec_spec,
            pl.BlockSpec((None, 2, N_MOD, D), lambda i: (i // bpb, 0, 0, 0)),
            vec_spec,
        ],
        out_specs=out_specs,
        out_shape=out_shape,
        compiler_params=_params(("arbitrary",)),
        name="postnorm",
    )(y2, x2, post_g.reshape(1, D), mod_pair, next_g.reshape(1, D))
    return (res[0], res[1]) if with_next else (res[0], None)


def _matmul_kernel(*refs, n_lhs, k_sizes, n_k, epilogue, n_cast):
    lhs_refs = refs[:n_lhs]
    w_ref = refs[n_lhs]
    n_in = n_lhs + 1 + n_cast
    o_ref = refs[n_in]
    for t in range(n_cast):
        refs[n_in + 1 + t][...] = refs[n_lhs + 1 + t][...].astype(BF16)
    acc = None
    off = 0
    for a_ref, ks in zip(lhs_refs, k_sizes):
        part = jnp.dot(a_ref[...], w_ref[off:off + ks, :], preferred_element_type=F32)
        acc = part if acc is None else acc + part
        off += ks

    def finish(r):
        if epilogue == "relu2":
            r = jnp.square(jnp.maximum(r, 0.0))
        o_ref[...] = r.astype(o_ref.dtype)

    if n_k == 1:
        finish(acc)
    else:
        acc_ref = refs[-1]
        k = pl.program_id(2)

        @pl.when(k == 0)
        def _():
            acc_ref[...] = acc

        @pl.when(k > 0)
        def _():
            acc_ref[...] += acc

        @pl.when(k == n_k - 1)
        def _():
            finish(acc_ref[...])


def _matmul(lhs_list, w, *, out_dtype, tm, tn, tk=None, col_offset=0, n_cols=None,
            epilogue=None, cast_srcs=(), name="matmul"):
    M = lhs_list[0].shape[0]
    K, N_total = w.shape
    N = N_total - col_offset if n_cols is None else n_cols
    k_sizes = tuple(a.shape[1] for a in lhs_list)
    assert sum(k_sizes) == K and M % tm == 0 and N % tn == 0 and col_offset % tn == 0
    joff = col_offset // tn
    if tk is None:
        n_k = 1
        in_specs = [pl.BlockSpec((tm, ks), lambda i, j: (i, 0)) for ks in k_sizes]
        in_specs.append(pl.BlockSpec((K, tn), lambda i, j: (0, j + joff)))
        out_specs = [pl.BlockSpec((tm, tn), lambda i, j: (i, j))]
        grid = (M // tm, N // tn)
        scratch = []
        kern_k_sizes = k_sizes
    else:
        assert len(lhs_list) == 1 and K % tk == 0
        n_k = K // tk
        in_specs = [pl.BlockSpec((tm, tk), lambda i, j, k: (i, k)),
                    pl.BlockSpec((tk, tn), lambda i, j, k: (k, j + joff))]
        out_specs = [pl.BlockSpec((tm, tn), lambda i, j, k: (i, j))]
        grid = (M // tm, N // tn, n_k)
        scratch = [pltpu.VMEM((tm, tn), F32)]
        kern_k_sizes = (tk,)
    out_shape = [jax.ShapeDtypeStruct((M, N), out_dtype)]
    operands = list(lhs_list) + [w]
    for src in cast_srcs:
        cast_in_spec, cast_out_spec, cast_shape = _cast_slab_specs(src, grid)
        in_specs.append(cast_in_spec)
        out_specs.append(cast_out_spec)
        out_shape.append(cast_shape)
        operands.append(src[0])
    res = pl.pallas_call(
        functools.partial(_matmul_kernel, n_lhs=len(lhs_list), k_sizes=kern_k_sizes,
                          n_k=n_k, epilogue=epilogue, n_cast=len(cast_srcs)),
        grid=grid,
        in_specs=in_specs,
        out_specs=out_specs,
        out_shape=out_shape,
        scratch_shapes=scratch,
        compiler_params=_params(("arbitrary",) * len(grid)),
        name=name,
    )(*operands)
    return res if cast_srcs else res[0]


def _bucket_thresholds():
    max_exact = N_BUCKETS // 2
    ths = []
    for b in range(max_exact + 1, N_BUCKETS):
        n = max_exact
        while True:
            v = max_exact + int(math.log(n / max_exact) / math.log(MAX_DISTANCE / max_exact)
                                * (N_BUCKETS - max_exact))
            if min(v, N_BUCKETS - 1) >= b:
                break
            n += 1
        ths.append(n)
    return ths


def _bias_kernel(rb_ref, o_ref):
    h = pl.program_id(0)
    T, C = ATTN_BLOCK, MAX_DISTANCE
    row = lax.broadcasted_iota(jnp.int32, (C, C), 0)
    col = lax.broadcasted_iota(jnp.int32, (C, C), 1)
    far = rb_ref[N_BUCKETS - 1, h]

    def band_tile(offset):
        d = row - col + offset
        n = jnp.maximum(d, 0)
        bucket = jnp.minimum(n, N_BUCKETS // 2)
        for th in _bucket_thresholds():
            bucket = bucket + (n >= th).astype(jnp.int32)
        tile = jnp.zeros((C, C), F32)
        for b in range(N_BUCKETS - 1):
            tile = jnp.where(bucket == b, (rb_ref[b, h] - far) * LOG2E, tile)
        return jnp.where(d >= 0, tile, -jnp.inf)

    on_diag = band_tile(0)
    below_diag = band_tile(C)
    zeros = jnp.zeros((C, C), F32)
    future = jnp.full((C, C), -jnp.inf, F32)
    nb = T // C
    for i in range(nb):
        for j in range(nb):
            sub = on_diag if i == j else below_diag if i == j + 1 else zeros if i > j else future
            o_ref[0, 0, i * C:(i + 1) * C, j * C:(j + 1) * C] = sub
            near = i == 0 and j == nb - 1
            o_ref[0, 1, i * C:(i + 1) * C, j * C:(j + 1) * C] = below_diag if near else zeros


def _bias_tiles(rel_bias):
    T = ATTN_BLOCK
    return pl.pallas_call(
        _bias_kernel,
        grid=(N_HEADS,),
        in_specs=[pl.BlockSpec(memory_space=pltpu.SMEM)],
        out_specs=pl.BlockSpec((1, 2, T, T), lambda h: (h, 0, 0, 0)),
        out_shape=jax.ShapeDtypeStruct((N_HEADS, 2, T, T), F32),
        compiler_params=_params(("arbitrary",)),
        name="bias_tiles",
    )(rel_bias)


def _lane_tile(x, n):
    return x if n == 1 else jnp.concatenate([x] * n, axis=1)


def _attn_kernel(q_ref, k_ref, v_ref, bias_ref, lam_ref, g_ref, cast_in_ref, o_ref,
                 cast_out_ref, m_ref, l_ref, acc_ref, *, lambda_init):
    T = ATTN_BLOCK
    qi = pl.program_id(2)
    scale2 = HEAD_DIM ** -0.5 * LOG2E

    cast_out_ref[...] = cast_in_ref[...].astype(BF16)
    m_ref[...] = jnp.full(m_ref.shape, -jnp.inf, F32)
    l_ref[...] = jnp.zeros(l_ref.shape, F32)
    acc_ref[...] = jnp.zeros(acc_ref.shape, F32)

    def step(j, bias):
        start = pl.multiple_of(j * T, T)
        k = k_ref[pl.ds(start, T), :]
        v = v_ref[pl.ds(start, T), :]
        for c in range(2):
            lo, hi = c * HEAD_DIM, (c + 1) * HEAD_DIM
            s = lax.dot_general(q_ref[:, lo:hi], k[:, lo:hi], (((1,), (1,)), ((), ())),
                                preferred_element_type=F32) * scale2
            if bias is not None:
                s = s + bias
            m_old = m_ref[c]
            m_new = jnp.maximum(m_old, jnp.max(s, axis=-1, keepdims=True))
            alpha = jnp.exp2(m_old - m_new)
            p = jnp.exp2(s - _lane_tile(m_new, T // LANES))
            p_part = p[:, :LANES]
            for t in range(1, T // LANES):
                p_part = p_part + p[:, t * LANES:(t + 1) * LANES]
            l_ref[c] = alpha * l_ref[c] + p_part
            acc_ref[c] = (_lane_tile(alpha, V_DIM // LANES) * acc_ref[c]
                          + jnp.dot(p.astype(BF16), v, preferred_element_type=F32))
            m_ref[c] = m_new

    def plain(j, carry):
        step(j, None)
        return carry

    lax.fori_loop(0, jnp.maximum(qi - 1, 0), plain, 0)

    @pl.when(qi >= 1)
    def _():
        step(qi - 1, bias_ref[0, 1])

    step(qi, bias_ref[0, 0])

    lam_v = lam_ref[...]
    lam = (jnp.exp(jnp.sum(lam_v[0:1] * lam_v[1:2], axis=-1, keepdims=True))
           - jnp.exp(jnp.sum(lam_v[2:3] * lam_v[3:4], axis=-1, keepdims=True))
           + lambda_init)
    l0 = jnp.sum(l_ref[0], axis=-1, keepdims=True)
    l1 = jnp.sum(l_ref[1], axis=-1, keepdims=True)
    out = acc_ref[0] / l0 - lam * (acc_ref[1] / l1)
    o_ref[...] = (_rms(out, g_ref[...]) * (1.0 - lambda_init)).astype(o_ref.dtype)


def _attention(qkv, bias_tiles, lam_vecs, subln_g, lambda_init, B, S, cast_src):
    T = ATTN_BLOCK
    nq = S // T
    grid = (B, N_HEADS, nq)
    cast_in_spec, cast_out_spec, cast_shape = _cast_slab_specs(cast_src, grid)
    return pl.pallas_call(
        functools.partial(_attn_kernel, lambda_init=lambda_init),
        grid=grid,
        in_specs=[
            pl.BlockSpec((T, V_DIM), lambda b, h, i: (b * nq + i, h)),
            pl.BlockSpec((S, V_DIM), lambda b, h, i: (b, N_HEADS + h)),
            pl.BlockSpec((S, V_DIM), lambda b, h, i: (b, 2 * N_HEADS + h)),
            pl.BlockSpec((1, 2, T, T), lambda b, h, i: (h, 0, 0, 0)),
            pl.BlockSpec((4, HEAD_DIM), lambda b, h, i: (0, 0)),
            pl.BlockSpec((1, V_DIM), lambda b, h, i: (0, 0)),
            cast_in_spec,
        ],
        out_specs=[pl.BlockSpec((T, V_DIM), lambda b, h, i: (b * nq + i, h)),
                   cast_out_spec],
        out_shape=[jax.ShapeDtypeStruct((B * S, ATTN_WIDTH), BF16), cast_shape],
        scratch_shapes=[pltpu.VMEM((2, T, LANES), F32), pltpu.VMEM((2, T, LANES), F32),
                        pltpu.VMEM((2, T, V_DIM), F32)],
        compiler_params=_params(("arbitrary", "arbitrary", "arbitrary")),
        name="diff_attention",
    )(qkv, qkv, qkv, bias_tiles, lam_vecs, subln_g.reshape(1, V_DIM), cast_src[0])


def _gate_kernel(z_ref, ws_ref, bs_ref, vg_ref, vb_ref, o_ref):
    z = z_ref[...]
    a = 0.5 * z * (1.0 + lax.erf(z * math.sqrt(0.5)))
    u = a[:, :GMLP_WIDTH]
    v = a[:, GMLP_WIDTH:]
    mu = jnp.mean(v, axis=-1, keepdims=True)
    vc = v - mu
    var = jnp.mean(vc * vc, axis=-1, keepdims=True)
    vn = (vc * lax.rsqrt(var + NORM_EPS) * vg_ref[...] + vb_ref[...]).astype(BF16)
    row = lax.broadcasted_iota(jnp.int32, (CHUNK, CHUNK), 0)
    col = lax.broadcasted_iota(jnp.int32, (CHUNK, CHUNK), 1)
    causal = row >= col
    bs = bs_ref[...]
    for g in range(N_GROUPS):
        ws = jnp.where(causal, ws_ref[g], 0.0).astype(BF16)
        cols = slice(g * GROUP_DIM, (g + 1) * GROUP_DIM)
        for n in range(z.shape[0] // CHUNK):
            rows = slice(n * CHUNK, (n + 1) * CHUNK)
            mixed = jnp.dot(ws, vn[rows, cols], preferred_element_type=F32) + bs[:, g:g + 1]
            o_ref[rows, cols] = (u[rows, cols] * mixed).astype(o_ref.dtype)


def _spatial_gate(z2, w_s, b_s, v_g, v_b):
    M = z2.shape[0]
    R = ROW_BLOCK
    return pl.pallas_call(
        _gate_kernel,
        grid=(M // R,),
        in_specs=[
            pl.BlockSpec((R, 2 * GMLP_WIDTH), lambda i: (i, 0)),
            pl.BlockSpec((N_GROUPS, CHUNK, CHUNK), lambda i: (0, 0, 0)),
            pl.BlockSpec((CHUNK, N_GROUPS), lambda i: (0, 0)),
            pl.BlockSpec((1, GMLP_WIDTH), lambda i: (0, 0)),
            pl.BlockSpec((1, GMLP_WIDTH), lambda i: (0, 0)),
        ],
        out_specs=pl.BlockSpec((R, GMLP_WIDTH), lambda i: (i, 0)),
        out_shape=jax.ShapeDtypeStruct((M, GMLP_WIDTH), BF16),
        compiler_params=_params(("arbitrary",)),
        name="spatial_gate",
    )(z2, w_s, b_s.T, v_g.reshape(1, GMLP_WIDTH), v_b.reshape(1, GMLP_WIDTH))


def kernel(x, c, rel_bias, w_ada, b_ada, pre_mix_g, w_in, lambda_q1, lambda_k1,
           lambda_q2, lambda_k2, subln_g, v_norm_g, v_norm_b, w_s, b_s, w_out,
           post_mix_g, pre_mlp_g, w_1, w_2, post_mlp_g):
    B, S, D = x.shape
    assert (D, x.dtype) == (D_MODEL, F32) and S % ATTN_BLOCK == 0 and S % ROW_BLOCK == 0
    M = B * S
    mod = _ada_mod(c, w_ada, b_ada)
    bias_tiles = _bias_tiles(rel_bias)
    x2 = x.reshape(M, D)
    h, w_in_b = _prenorm(x2, pre_mix_g[0], mod[0], 0, S, cast_src=(w_in, 0))
    for l in range(DEPTH):
        lambda_init = 0.8 - 0.6 * math.exp(-0.3 * l)
        qkv = _matmul([h], w_in_b, out_dtype=BF16, tm=1024, tn=1024, n_cols=QKV_COLS,
                      name="in_proj_qkv")
        z, w_out_b = _matmul([h], w_in_b, out_dtype=F32, tm=1024, tn=1024,
                             col_offset=QKV_COLS, cast_srcs=((w_out, l),), name="in_proj_z")
        lam_vecs = jnp.stack([lambda_q1[l], lambda_k1[l], lambda_q2[l], lambda_k2[l]])
        attn, w_1_b = _attention(qkv, bias_tiles, lam_vecs, subln_g[l], lambda_init, B, S,
                                 cast_src=(w_1, l))
        gm = _spatial_gate(z, w_s[l], b_s[l], v_norm_g[l], v_norm_b[l])
        y = _matmul([attn, gm], w_out_b, out_dtype=F32, tm=1024, tn=1024, name="out_proj")
        mod_same = jnp.stack([mod[l], mod[l]], axis=1)
        x2, h = _postnorm(y, x2, post_mix_g[l], mod_same, pre_mlp_g[l], 2, 3, S)
        next_w_in = ((w_in, l + 1),) if l + 1 < DEPTH else ()
        a, w_2_b, *w_in_next_b = _matmul([h], w_1_b, out_dtype=BF16, tm=1024, tn=1024,
                                         epilogue="relu2", cast_srcs=((w_2, l),) + next_w_in,
                                         name="mlp_up")
        y = _matmul([a], w_2_b, out_dtype=F32, tm=1024, tn=1024, tk=4096, name="mlp_down")
        if l + 1 < DEPTH:
            w_in_b = w_in_next_b[0]
            mod_next = jnp.stack([mod[l], mod[l + 1]], axis=1)
            x2, h = _postnorm(y, x2, post_mlp_g[l], mod_next, pre_mix_g[l + 1], 5, 0, S)
        else:
            mod_same = jnp.stack([mod[l], mod[l]], axis=1)
            x2, _ = _postnorm(y, x2, post_mlp_g[l], mod_same, post_mlp_g[l], 5, None, S)
    return x2.reshape(B, S, D)
```

```python
import functools
import math

import jax
import jax.numpy as jnp
from jax import lax
from jax.experimental import pallas as pl
from jax.experimental.pallas import tpu as pltpu

F32 = jnp.float32
BF16 = jnp.bfloat16

D_MODEL = 4096
DEPTH = 2
ATTN_WIDTH = 2048
GMLP_WIDTH = 2048
HEAD_DIM = 128
V_DIM = 2 * HEAD_DIM
N_HEADS = ATTN_WIDTH // V_DIM
N_GROUPS = 8
GROUP_DIM = GMLP_WIDTH // N_GROUPS
CHUNK = 128
D_FF = 4 * D_MODEL
N_BUCKETS = 32
MAX_DISTANCE = 128
NORM_EPS = 1e-6
N_MOD = 6
QKV_COLS = 3 * ATTN_WIDTH
IN_COLS = QKV_COLS + 2 * GMLP_WIDTH

V7X_VMEM_BYTES = 64 * 1024 * 1024
VMEM_LIMIT_BYTES = V7X_VMEM_BYTES - 8 * 1024 * 1024
SUBLANES = 8
BF16_SUBLANES = 16
LANES = 128
LOG2E = math.log2(math.e)

ATTN_BLOCK = 512
ROW_BLOCK = 256


def _params(semantics):
    return pltpu.CompilerParams(dimension_semantics=semantics,
                                vmem_limit_bytes=VMEM_LIMIT_BYTES)


def _ada_kernel(c_ref, w_ref, b_ref, o_ref):
    c = c_ref[...]
    c_act = (c * jax.nn.sigmoid(c)).astype(BF16)
    w = w_ref[0].astype(BF16)
    o_ref[0] = jnp.dot(c_act, w, preferred_element_type=F32) + b_ref[0]


def _ada_mod(c, w_ada, b_ada):
    B, D = c.shape
    L, _, N = w_ada.shape
    tn = 512
    c_pad = jnp.zeros((SUBLANES, D), F32).at[:B].set(c)
    out = pl.pallas_call(
        _ada_kernel,
        grid=(L, N // tn),
        in_specs=[
            pl.BlockSpec((SUBLANES, D), lambda l, j: (0, 0)),
            pl.BlockSpec((1, D, tn), lambda l, j: (l, 0, j)),
            pl.BlockSpec((1, 1, tn), lambda l, j: (l, 0, j)),
        ],
        out_specs=pl.BlockSpec((1, SUBLANES, tn), lambda l, j: (l, 0, j)),
        out_shape=jax.ShapeDtypeStruct((L, SUBLANES, N), F32),
        compiler_params=_params(("arbitrary", "arbitrary")),
        name="ada_mod",
    )(c_pad, w_ada, b_ada.reshape(L, 1, N))
    return out[:, :B].reshape(L, B, N_MOD, D)


def _cast_slab_specs(cast_src, grid):
    stacked, layer = cast_src
    n_steps = math.prod(grid)
    _, rows, cols = stacked.shape
    assert rows % (n_steps * BF16_SUBLANES) == 0
    slab = rows // n_steps
    strides = [math.prod(grid[d + 1:]) for d in range(len(grid))]

    def step(*g):
        return sum(gi * st for gi, st in zip(g, strides))

    return (pl.BlockSpec((None, slab, cols), lambda *g: (layer, step(*g), 0)),
            pl.BlockSpec((slab, cols), lambda *g: (step(*g), 0)),
            jax.ShapeDtypeStruct((rows, cols), BF16))


def _rms(x, g):
    ms = jnp.mean(x * x, axis=-1, keepdims=True)
    return x * lax.rsqrt(ms + NORM_EPS) * g


def _prenorm_kernel(x_ref, g_ref, mod_ref, cast_in_ref, h_ref, cast_out_ref, *, shift_idx):
    cast_out_ref[...] = cast_in_ref[...].astype(BF16)
    shift = mod_ref[0, shift_idx:shift_idx + 1, :]
    scale = mod_ref[0, shift_idx + 1:shift_idx + 2, :]
    h = _rms(x_ref[...], g_ref[...]) * (1.0 + scale) + shift
    h_ref[...] = h.astype(BF16)


def _prenorm(x2, g, mod, shift_idx, rows_per_batch, cast_src):
    M, D = x2.shape
    tm = ROW_BLOCK
    bpb = rows_per_batch // tm
    grid = (M // tm,)
    cast_in_spec, cast_out_spec, cast_shape = _cast_slab_specs(cast_src, grid)
    return pl.pallas_call(
        functools.partial(_prenorm_kernel, shift_idx=shift_idx),
        grid=grid,
        in_specs=[
            pl.BlockSpec((tm, D), lambda i: (i, 0)),
            pl.BlockSpec((1, D), lambda i: (0, 0)),
            pl.BlockSpec((1, N_MOD, D), lambda i: (i // bpb, 0, 0)),
            cast_in_spec,
        ],
        out_specs=[pl.BlockSpec((tm, D), lambda i: (i, 0)), cast_out_spec],
        out_shape=[jax.ShapeDtypeStruct((M, D), BF16), cast_shape],
        compiler_params=_params(("arbitrary",)),
        name="prenorm",
    )(x2, g.reshape(1, D), mod, cast_src[0])


def _postnorm_kernel(y_ref, x_ref, pg_ref, mod_ref, ng_ref, xo_ref, h_ref=None, *,
                     gate_idx, next_shift_idx):
    gate = mod_ref[0, gate_idx:gate_idx + 1, :]
    x_new = x_ref[...] + gate * _rms(y_ref[...], pg_ref[...])
    xo_ref[...] = x_new
    if next_shift_idx is not None:
        shift = mod_ref[1, next_shift_idx:next_shift_idx + 1, :]
        scale = mod_ref[1, next_shift_idx + 1:next_shift_idx + 2, :]
        h = _rms(x_new, ng_ref[...]) * (1.0 + scale) + shift
        h_ref[...] = h.astype(BF16)


def _postnorm(y2, x2, post_g, mod_pair, next_g, gate_idx, next_shift_idx, rows_per_batch):
    M, D = x2.shape
    tm = ROW_BLOCK
    bpb = rows_per_batch // tm
    row_spec = pl.BlockSpec((tm, D), lambda i: (i, 0))
    vec_spec = pl.BlockSpec((1, D), lambda i: (0, 0))
    with_next = next_shift_idx is not None
    kern = functools.partial(_postnorm_kernel, gate_idx=gate_idx,
                             next_shift_idx=next_shift_idx)
    out_shape = [jax.ShapeDtypeStruct((M, D), F32)]
    out_specs = [row_spec]
    if with_next:
        out_shape.append(jax.ShapeDtypeStruct((M, D), BF16))
        out_specs.append(row_spec)
    res = pl.pallas_call(
        kern,
        grid=(M // tm,),
        in_specs=[
            row_spec, row_spec, vec_spec,
            pl.BlockSpec((None, 2, N_MOD, D), lambda i: (i // bpb, 0, 0, 0)),
            vec_spec,
        ],
        out_specs=out_specs,
        out_shape=out_shape,
        compiler_params=_params(("arbitrary",)),
        name="postnorm",
    )(y2, x2, post_g.reshape(1, D), mod_pair, next_g.reshape(1, D))
    return (res[0], res[1]) if with_next else (res[0], None)


def _matmul_kernel(*refs, n_lhs, k_sizes, n_k, epilogue, n_cast):
    lhs_refs = refs[:n_lhs]
    w_ref = refs[n_lhs]
    n_in = n_lhs + 1 + n_cast
    o_ref = refs[n_in]
    for t in range(n_cast):
        refs[n_in + 1 + t][...] = refs[n_lhs + 1 + t][...].astype(BF16)
    acc = None
    off = 0
    for a_ref, ks in zip(lhs_refs, k_sizes):
        part = jnp.dot(a_ref[...], w_ref[off:off + ks, :], preferred_element_type=F32)
        acc = part if acc is None else acc + part
        off += ks

    if n_k == 1:
        if epilogue == "relu2":
            acc = jnp.square(jnp.maximum(acc, 0.0))
        o_ref[...] = acc.astype(o_ref.dtype)
    else:
        k = pl.program_id(2)

        @pl.when(k == 0)
        def _():
            o_ref[...] = acc

        @pl.when(k > 0)
        def _():
            o_ref[...] += acc


def _matmul(lhs_list, w, *, out_dtype, tm, tn, tk=None, col_offset=0, n_cols=None,
            epilogue=None, cast_srcs=(), name="matmul"):
    M = lhs_list[0].shape[0]
    K, N_total = w.shape
    N = N_total - col_offset if n_cols is None else n_cols
    k_sizes = tuple(a.shape[1] for a in lhs_list)
    assert sum(k_sizes) == K and M % tm == 0 and N % tn == 0 and col_offset % tn == 0
    joff = col_offset // tn
    if tk is None:
        n_k = 1
        in_specs = [pl.BlockSpec((tm, ks), lambda i, j: (i, 0)) for ks in k_sizes]
        in_specs.append(pl.BlockSpec((K, tn), lambda i, j: (0, j + joff)))
        out_specs = [pl.BlockSpec((tm, tn), lambda i, j: (i, j))]
        grid = (M // tm, N // tn)
        kern_k_sizes = k_sizes
    else:
        assert len(lhs_list) == 1 and K % tk == 0 and out_dtype == F32 and epilogue is None
        n_k = K // tk
        in_specs = [pl.BlockSpec((tm, tk), lambda i, j, k: (i, k)),
                    pl.BlockSpec((tk, tn), lambda i, j, k: (k, j + joff))]
        out_specs = [pl.BlockSpec((tm, tn), lambda i, j, k: (i, j))]
        grid = (M // tm, N // tn, n_k)
        kern_k_sizes = (tk,)
    out_shape = [jax.ShapeDtypeStruct((M, N), out_dtype)]
    operands = list(lhs_list) + [w]
    for src in cast_srcs:
        cast_in_spec, cast_out_spec, cast_shape = _cast_slab_specs(src, grid)
        in_specs.append(cast_in_spec)
        out_specs.append(cast_out_spec)
        out_shape.append(cast_shape)
        operands.append(src[0])
    res = pl.pallas_call(
        functools.partial(_matmul_kernel, n_lhs=len(lhs_list), k_sizes=kern_k_sizes,
                          n_k=n_k, epilogue=epilogue, n_cast=len(cast_srcs)),
        grid=grid,
        in_specs=in_specs,
        out_specs=out_specs,
        out_shape=out_shape,
        compiler_params=_params(("arbitrary",) * len(grid)),
        name=name,
    )(*operands)
    return res if cast_srcs else res[0]


def _bucket_thresholds():
    max_exact = N_BUCKETS // 2
    ths = []
    for b in range(max_exact + 1, N_BUCKETS):
        n = max_exact
        while True:
            v = max_exact + int(math.log(n / max_exact) / math.log(MAX_DISTANCE / max_exact)
                                * (N_BUCKETS - max_exact))
            if min(v, N_BUCKETS - 1) >= b:
                break
            n += 1
        ths.append(n)
    return ths


def _bias_kernel(rb_ref, o_ref):
    h = pl.program_id(0)
    T, C = ATTN_BLOCK, MAX_DISTANCE
    row = lax.broadcasted_iota(jnp.int32, (C, C), 0)
    col = lax.broadcasted_iota(jnp.int32, (C, C), 1)
    far = rb_ref[N_BUCKETS - 1, h]

    def band_tile(offset):
        d = row - col + offset
        n = jnp.maximum(d, 0)
        bucket = jnp.minimum(n, N_BUCKETS // 2)
        for th in _bucket_thresholds():
            bucket = bucket + (n >= th).astype(jnp.int32)
        tile = jnp.zeros((C, C), F32)
        for b in range(N_BUCKETS - 1):
            tile = jnp.where(bucket == b, (rb_ref[b, h] - far) * LOG2E, tile)
        return jnp.where(d >= 0, tile, -jnp.inf)

    on_diag = band_tile(0)
    below_diag = band_tile(C)
    zeros = jnp.zeros((C, C), F32)
    future = jnp.full((C, C), -jnp.inf, F32)
    nb = T // C
    for i in range(nb):
        for j in range(nb):
            sub = on_diag if i == j else below_diag if i == j + 1 else zeros if i > j else future
            o_ref[0, 0, i * C:(i + 1) * C, j * C:(j + 1) * C] = sub
            near = i == 0 and j == nb - 1
            o_ref[0, 1, i * C:(i + 1) * C, j * C:(j + 1) * C] = below_diag if near else zeros


def _bias_tiles(rel_bias):
    T = ATTN_BLOCK
    return pl.pallas_call(
        _bias_kernel,
        grid=(N_HEADS,),
        in_specs=[pl.BlockSpec(memory_space=pltpu.SMEM)],
        out_specs=pl.BlockSpec((1, 2, T, T), lambda h: (h, 0, 0, 0)),
        out_shape=jax.ShapeDtypeStruct((N_HEADS, 2, T, T), F32),
        compiler_params=_params(("arbitrary",)),
        name="bias_tiles",
    )(rel_bias)


def _lane_tile(x, n):
    return x if n == 1 else jnp.concatenate([x] * n, axis=1)


def _attn_kernel(q_ref, k_ref, v_ref, bias_ref, lam_ref, g_ref, cast_in_ref, o_ref,
                 cast_out_ref, m_ref, l_ref, acc_ref, sa_ref, sb_ref, *, lambda_init):
    T = ATTN_BLOCK
    qi = pl.program_id(2)
    scale2 = HEAD_DIM ** -0.5 * LOG2E

    cast_out_ref[...] = cast_in_ref[...].astype(BF16)
    m_ref[...] = jnp.full(m_ref.shape, -jnp.inf, F32)
    l_ref[...] = jnp.zeros(l_ref.shape, F32)
    acc_ref[...] = jnp.zeros(acc_ref.shape, F32)

    def scores(j, s_ref):
        start = pl.multiple_of(j * T, T)
        k = k_ref[pl.ds(start, T), :]
        for c in range(2):
            lo, hi = c * HEAD_DIM, (c + 1) * HEAD_DIM
            s_ref[c] = lax.dot_general(
                q_ref[:, lo:hi], k[:, lo:hi], (((1,), (1,)), ((), ())),
                preferred_element_type=F32) * scale2

    def consume(j, s_ref, bias):
        start = pl.multiple_of(j * T, T)
        v = v_ref[pl.ds(start, T), :]
        for c in range(2):
            s = s_ref[c]
            if bias is not None:
                s = s + bias
            m_old = m_ref[c]
            m_new = jnp.maximum(m_old, jnp.max(s, axis=-1, keepdims=True))
            alpha = jnp.exp2(m_old - m_new)
            p = jnp.exp2(s - _lane_tile(m_new, T // LANES))
            p_part = p[:, :LANES]
            for t in range(1, T // LANES):
                p_part = p_part + p[:, t * LANES:(t + 1) * LANES]
            l_ref[c] = alpha * l_ref[c] + p_part
            acc_ref[c] = (_lane_tile(alpha, V_DIM // LANES) * acc_ref[c]
                          + jnp.dot(p.astype(BF16), v, preferred_element_type=F32))
            m_ref[c] = m_new

    scores(0, sa_ref)

    def plain_pair(i, carry):
        j = 2 * i
        scores(j + 1, sb_ref)
        consume(j, sa_ref, None)
        scores(j + 2, sa_ref)
        consume(j + 1, sb_ref, None)
        return carry

    n_plain = jnp.maximum(qi - 1, 0)
    lax.fori_loop(0, n_plain // 2, plain_pair, 0)

    @pl.when(qi == 0)
    def _():
        consume(0, sa_ref, bias_ref[0, 0])

    @pl.when(qi % 2 == 1)
    def _():
        scores(qi, sb_ref)
        consume(qi - 1, sa_ref, bias_ref[0, 1])
        consume(qi, sb_ref, bias_ref[0, 0])

    @pl.when(jnp.logical_and(qi % 2 == 0, qi >= 2))
    def _():
        scores(qi - 1, sb_ref)
        consume(qi - 2, sa_ref, None)
        scores(qi, sa_ref)
        consume(qi - 1, sb_ref, bias_ref[0, 1])
        consume(qi, sa_ref, bias_ref[0, 0])

    lam_v = lam_ref[...]
    lam = (jnp.exp(jnp.sum(lam_v[0:1] * lam_v[1:2], axis=-1, keepdims=True))
           - jnp.exp(jnp.sum(lam_v[2:3] * lam_v[3:4], axis=-1, keepdims=True))
           + lambda_init)
    l0 = jnp.sum(l_ref[0], axis=-1, keepdims=True)
    l1 = jnp.sum(l_ref[1], axis=-1, keepdims=True)
    out = acc_ref[0] / l0 - lam * (acc_ref[1] / l1)
    o_ref[...] = (_rms(out, g_ref[...]) * (1.0 - lambda_init)).astype(o_ref.dtype)


def _attention(qkv, bias_tiles, lam_vecs, subln_g, lambda_init, B, S, cast_src):
    T = ATTN_BLOCK
    nq = S // T
    grid = (B, N_HEADS, nq)
    cast_in_spec, cast_out_spec, cast_shape = _cast_slab_specs(cast_src, grid)
    return pl.pallas_call(
        functools.partial(_attn_kernel, lambda_init=lambda_init),
        grid=grid,
        in_specs=[
            pl.BlockSpec((T, V_DIM), lambda b, h, i: (b * nq + i, h)),
            pl.BlockSpec((S, V_DIM), lambda b, h, i: (b, N_HEADS + h)),
            pl.BlockSpec((S, V_DIM), lambda b, h, i: (b, 2 * N_HEADS + h)),
            pl.BlockSpec((1, 2, T, T), lambda b, h, i: (h, 0, 0, 0)),
            pl.BlockSpec((4, HEAD_DIM), lambda b, h, i: (0, 0)),
            pl.BlockSpec((1, V_DIM), lambda b, h, i: (0, 0)),
            cast_in_spec,
        ],
        out_specs=[pl.BlockSpec((T, V_DIM), lambda b, h, i: (b * nq + i, h)),
                   cast_out_spec],
        out_shape=[jax.ShapeDtypeStruct((B * S, ATTN_WIDTH), BF16), cast_shape],
        scratch_shapes=[pltpu.VMEM((2, T, LANES), F32), pltpu.VMEM((2, T, LANES), F32),
                        pltpu.VMEM((2, T, V_DIM), F32), pltpu.VMEM((2, T, T), F32),
                        pltpu.VMEM((2, T, T), F32)],
        compiler_params=_params(("arbitrary", "arbitrary", "arbitrary")),
        name="diff_attention",
    )(qkv, qkv, qkv, bias_tiles, lam_vecs, subln_g.reshape(1, V_DIM), cast_src[0])


def _gate_kernel(z_ref, ws_ref, bs_ref, vg_ref, vb_ref, o_ref):
    z = z_ref[...]
    a = 0.5 * z * (1.0 + lax.erf(z * math.sqrt(0.5)))
    u = a[:, :GMLP_WIDTH]
    v = a[:, GMLP_WIDTH:]
    mu = jnp.mean(v, axis=-1, keepdims=True)
    vc = v - mu
    var = jnp.mean(vc * vc, axis=-1, keepdims=True)
    vn = (vc * lax.rsqrt(var + NORM_EPS) * vg_ref[...] + vb_ref[...]).astype(BF16)
    row = lax.broadcasted_iota(jnp.int32, (CHUNK, CHUNK), 0)
    col = lax.broadcasted_iota(jnp.int32, (CHUNK, CHUNK), 1)
    causal = row >= col
    bs = bs_ref[...]
    for g in range(N_GROUPS):
        ws = jnp.where(causal, ws_ref[g], 0.0).astype(BF16)
        cols = slice(g * GROUP_DIM, (g + 1) * GROUP_DIM)
        for n in range(z.shape[0] // CHUNK):
            rows = slice(n * CHUNK, (n + 1) * CHUNK)
            mixed = jnp.dot(ws, vn[rows, cols], preferred_element_type=F32) + bs[:, g:g + 1]
            o_ref[rows, cols] = (u[rows, cols] * mixed).astype(o_ref.dtype)


def _spatial_gate(z2, w_s, b_s, v_g, v_b):
    M = z2.shape[0]
    R = ROW_BLOCK
    return pl.pallas_call(
        _gate_kernel,
        grid=(M // R,),
        in_specs=[
            pl.BlockSpec((R, 2 * GMLP_WIDTH), lambda i: (i, 0)),
            pl.BlockSpec((N_GROUPS, CHUNK, CHUNK), lambda i: (0, 0, 0)),
            pl.BlockSpec((CHUNK, N_GROUPS), lambda i: (0, 0)),
            pl.BlockSpec((1, GMLP_WIDTH), lambda i: (0, 0)),
            pl.BlockSpec((1, GMLP_WIDTH), lambda i: (0, 0)),
        ],
        out_specs=pl.BlockSpec((R, GMLP_WIDTH), lambda i: (i, 0)),
        out_shape=jax.ShapeDtypeStruct((M, GMLP_WIDTH), BF16),
        compiler_params=_params(("arbitrary",)),
        name="spatial_gate",
    )(z2, w_s, b_s.T, v_g.reshape(1, GMLP_WIDTH), v_b.reshape(1, GMLP_WIDTH))


def kernel(x, c, rel_bias, w_ada, b_ada, pre_mix_g, w_in, lambda_q1, lambda_k1,
           lambda_q2, lambda_k2, subln_g, v_norm_g, v_norm_b, w_s, b_s, w_out,
           post_mix_g, pre_mlp_g, w_1, w_2, post_mlp_g):
    B, S, D = x.shape
    assert (D, x.dtype) == (D_MODEL, F32) and S % ATTN_BLOCK == 0 and S % ROW_BLOCK == 0
    M = B * S
    mod = _ada_mod(c, w_ada, b_ada)
    bias_tiles = _bias_tiles(rel_bias)
    x2 = x.reshape(M, D)
    h, w_in_b = _prenorm(x2, pre_mix_g[0], mod[0], 0, S, cast_src=(w_in, 0))
    for l in range(DEPTH):
        lambda_init = 0.8 - 0.6 * math.exp(-0.3 * l)
        qkv = _matmul([h], w_in_b, out_dtype=BF16, tm=1024, tn=1024, n_cols=QKV_COLS,
                      name="in_proj_qkv")
        z, w_out_b = _matmul([h], w_in_b, out_dtype=F32, tm=1024, tn=1024,
                             col_offset=QKV_COLS, cast_srcs=((w_out, l),), name="in_proj_z")
        lam_vecs = jnp.stack([lambda_q1[l], lambda_k1[l], lambda_q2[l], lambda_k2[l]])
        attn, w_1_b = _attention(qkv, bias_tiles, lam_vecs, subln_g[l], lambda_init, B, S,
                                 cast_src=(w_1, l))
        gm = _spatial_gate(z, w_s[l], b_s[l], v_norm_g[l], v_norm_b[l])
        y = _matmul([attn, gm], w_out_b, out_dtype=F32, tm=1024, tn=1024, name="out_proj")
        mod_same = jnp.stack([mod[l], mod[l]], axis=1)
        x2, h = _postnorm(y, x2, post_mix_g[l], mod_same, pre_mlp_g[l], 2, 3, S)
        next_w_in = ((w_in, l + 1),) if l + 1 < DEPTH else ()
        a, w_2_b, *w_in_next_b = _matmul([h], w_1_b, out_dtype=BF16, tm=1024, tn=1024,
                                         epilogue="relu2", cast_srcs=((w_2, l),) + next_w_in,
                                         name="mlp_up")
        y = _matmul([a], w_2_b, out_dtype=F32, tm=1024, tn=1024, tk=4096, name="mlp_down")
        if l + 1 < DEPTH:
            w_in_b = w_in_next_b[0]
            mod_next = jnp.stack([mod[l], mod[l + 1]], axis=1)
            x2, h = _postnorm(y, x2, post_mlp_g[l], mod_next, pre_mix_g[l + 1], 5, 0, S)
        else:
            mod_same = jnp.stack([mod[l], mod[l]], axis=1)
            x2, _ = _postnorm(y, x2, post_mlp_g[l], mod_same, post_mlp_g[l], 5, None, S)
    return x2.reshape(B, S, D)
```

```python
import functools
import math

import jax
import jax.numpy as jnp
from jax import lax
from jax.experimental import pallas as pl
from jax.experimental.pallas import tpu as pltpu

F32 = jnp.float32
BF16 = jnp.bfloat16

D_MODEL = 4096
DEPTH = 2
ATTN_WIDTH = 2048
GMLP_WIDTH = 2048
HEAD_DIM = 128
V_DIM = 2 * HEAD_DIM
N_HEADS = ATTN_WIDTH // V_DIM
N_GROUPS = 8
GROUP_DIM = GMLP_WIDTH // N_GROUPS
CHUNK = 128
D_FF = 4 * D_MODEL
N_BUCKETS = 32
MAX_DISTANCE = 128
NORM_EPS = 1e-6
N_MOD = 6
QKV_COLS = 3 * ATTN_WIDTH
IN_COLS = QKV_COLS + 2 * GMLP_WIDTH

V7X_VMEM_BYTES = 64 * 1024 * 1024
VMEM_LIMIT_BYTES = V7X_VMEM_BYTES - 4 * 1024 * 1024
SUBLANES = 8
BF16_SUBLANES = 16
LANES = 128
LOG2E = math.log2(math.e)

ATTN_BLOCK = 512
ROW_BLOCK = 256
NORM_SLAB = 64


def _params(semantics):
    return pltpu.CompilerParams(dimension_semantics=semantics,
                                vmem_limit_bytes=VMEM_LIMIT_BYTES)


def _ada_kernel(c_ref, w_ref, b_ref, o_ref):
    c = c_ref[...]
    c_act = (c * jax.nn.sigmoid(c)).astype(BF16)
    w = w_ref[0].astype(BF16)
    o_ref[0] = jnp.dot(c_act, w, preferred_element_type=F32) + b_ref[0]


def _ada_mod(c, w_ada, b_ada):
    B, D = c.shape
    L, _, N = w_ada.shape
    tn = 512
    c_pad = jnp.zeros((SUBLANES, D), F32).at[:B].set(c)
    out = pl.pallas_call(
        _ada_kernel,
        grid=(L, N // tn),
        in_specs=[
            pl.BlockSpec((SUBLANES, D), lambda l, j: (0, 0)),
            pl.BlockSpec((1, D, tn), lambda l, j: (l, 0, j)),
            pl.BlockSpec((1, 1, tn), lambda l, j: (l, 0, j)),
        ],
        out_specs=pl.BlockSpec((1, SUBLANES, tn), lambda l, j: (l, 0, j)),
        out_shape=jax.ShapeDtypeStruct((L, SUBLANES, N), F32),
        compiler_params=_params(("arbitrary", "arbitrary")),
        name="ada_mod",
    )(c_pad, w_ada, b_ada.reshape(L, 1, N))
    return out[:, :B].reshape(L, B, N_MOD, D)


def _cast_slab_specs(cast_src, grid, n_slabs=None):
    stacked, layer = cast_src
    n_steps = math.prod(grid)
    n_slabs = n_steps if n_slabs is None else n_slabs
    _, rows, cols = stacked.shape
    assert rows % (n_slabs * BF16_SUBLANES) == 0 and n_slabs <= n_steps
    slab = rows // n_slabs
    strides = [math.prod(grid[d + 1:]) for d in range(len(grid))]

    def step(*g):
        t = sum(gi * st for gi, st in zip(g, strides))
        return t if n_slabs == n_steps else jnp.minimum(t, n_slabs - 1)

    return (pl.BlockSpec((None, slab, cols), lambda *g: (layer, step(*g), 0)),
            pl.BlockSpec((slab, cols), lambda *g: (step(*g), 0)),
            jax.ShapeDtypeStruct((rows, cols), BF16))


def _rms(x, g):
    ms = jnp.mean(x * x, axis=-1, keepdims=True)
    return x * lax.rsqrt(ms + NORM_EPS) * g


def _prenorm_kernel(x_ref, g_ref, mod_ref, cast_in_ref, h_ref, cast_out_ref, *, shift_idx):
    cast_out_ref[...] = cast_in_ref[...].astype(BF16)
    shift = mod_ref[0, shift_idx:shift_idx + 1, :]
    scale = mod_ref[0, shift_idx + 1:shift_idx + 2, :]
    h = _rms(x_ref[...], g_ref[...]) * (1.0 + scale) + shift
    h_ref[...] = h.astype(BF16)


def _prenorm(x2, g, mod, shift_idx, rows_per_batch, cast_src):
    M, D = x2.shape
    tm = ROW_BLOCK
    bpb = rows_per_batch // tm
    grid = (M // tm,)
    cast_in_spec, cast_out_spec, cast_shape = _cast_slab_specs(cast_src, grid)
    return pl.pallas_call(
        functools.partial(_prenorm_kernel, shift_idx=shift_idx),
        grid=grid,
        in_specs=[
            pl.BlockSpec((tm, D), lambda i: (i, 0)),
            pl.BlockSpec((1, D), lambda i: (0, 0)),
            pl.BlockSpec((1, N_MOD, D), lambda i: (i // bpb, 0, 0)),
            cast_in_spec,
        ],
        out_specs=[pl.BlockSpec((tm, D), lambda i: (i, 0)), cast_out_spec],
        out_shape=[jax.ShapeDtypeStruct((M, D), BF16), cast_shape],
        compiler_params=_params(("arbitrary",)),
        name="prenorm",
    )(x2, g.reshape(1, D), mod, cast_src[0])


def _postnorm_kernel(y_ref, x_ref, pg_ref, mod_ref, ng_ref, xo_ref, h_ref=None, *,
                     gate_idx, next_shift_idx):
    gate = mod_ref[0, gate_idx:gate_idx + 1, :]
    x_new = x_ref[...] + gate * _rms(y_ref[...], pg_ref[...])
    xo_ref[...] = x_new
    if next_shift_idx is not None:
        shift = mod_ref[1, next_shift_idx:next_shift_idx + 1, :]
        scale = mod_ref[1, next_shift_idx + 1:next_shift_idx + 2, :]
        h = _rms(x_new, ng_ref[...]) * (1.0 + scale) + shift
        h_ref[...] = h.astype(BF16)


def _postnorm(y2, x2, post_g, mod_pair, next_g, gate_idx, next_shift_idx, rows_per_batch):
    M, D = x2.shape
    tm = ROW_BLOCK
    bpb = rows_per_batch // tm
    row_spec = pl.BlockSpec((tm, D), lambda i: (i, 0))
    vec_spec = pl.BlockSpec((1, D), lambda i: (0, 0))
    with_next = next_shift_idx is not None
    kern = functools.partial(_postnorm_kernel, gate_idx=gate_idx,
                             next_shift_idx=next_shift_idx)
    out_shape = [jax.ShapeDtypeStruct((M, D), F32)]
    out_specs = [row_spec]
    if with_next:
        out_shape.append(jax.ShapeDtypeStruct((M, D), BF16))
        out_specs.append(row_spec)
    res = pl.pallas_call(
        kern,
        grid=(M // tm,),
        in_specs=[
            row_spec, row_spec, vec_spec,
            pl.BlockSpec((None, 2, N_MOD, D), lambda i: (i // bpb, 0, 0, 0)),
            vec_spec,
        ],
        out_specs=out_specs,
        out_shape=out_shape,
        compiler_params=_params(("arbitrary",)),
        name="postnorm",
    )(y2, x2, post_g.reshape(1, D), mod_pair, next_g.reshape(1, D))
    return (res[0], res[1]) if with_next else (res[0], None)


def _norm_matmul_kernel(y_ref, x_ref, pg_ref, mod_ref, ng_ref, w_ref, *rest, n_i, n_cast,
                        gate_idx, next_shift_idx, epilogue):
    cast_in = rest[:n_cast]
    o_ref, xo_ref = rest[n_cast], rest[n_cast + 1]
    cast_out = rest[n_cast + 2:2 * n_cast + 2]
    h_refs = rest[2 * n_cast + 2:]
    i = pl.program_id(0)
    j = pl.program_id(1)
    for src, dst in zip(cast_in, cast_out):
        dst[...] = src[...].astype(BF16)

    def norm_slab(h_ref):
        gate = mod_ref[0, gate_idx:gate_idx + 1, :]
        shift = mod_ref[1, next_shift_idx:next_shift_idx + 1, :]
        scale = mod_ref[1, next_shift_idx + 1:next_shift_idx + 2, :]
        x_new = x_ref[...] + gate * _rms(y_ref[...], pg_ref[...])
        xo_ref[...] = x_new
        h = _rms(x_new, ng_ref[...]) * (1.0 + scale) + shift
        rows = pl.ds(pl.multiple_of(j * NORM_SLAB, NORM_SLAB), NORM_SLAB)
        h_ref[rows, :] = h.astype(BF16)

    def matmul(h_ref):
        r = jnp.dot(h_ref[...], w_ref[...], preferred_element_type=F32)
        if epilogue == "relu2":
            r = jnp.square(jnp.maximum(r, 0.0))
        o_ref[...] = r.astype(o_ref.dtype)

    @pl.when(i == 0)
    def _():
        norm_slab(h_refs[0])

    for parity in (0, 1):
        @pl.when(jnp.logical_and(jnp.logical_and(i >= 1, i < n_i), i % 2 == parity))
        def _():
            matmul(h_refs[1 - parity])
            norm_slab(h_refs[parity])

    @pl.when(i == n_i)
    def _():
        matmul(h_refs[(n_i - 1) % 2])


def _norm_matmul(y2, x2, post_g, mod_pair, next_g, gate_idx, next_shift_idx, rows_per_batch,
                 w, *, out_dtype, tm, tn, epilogue=None, cast_srcs=(), name="norm_matmul"):
    M, D = x2.shape
    N = w.shape[1]
    n_i, n_j = M // tm, N // tn
    assert M % tm == 0 and N % tn == 0 and tm == n_j * NORM_SLAB
    n_slabs = n_i * n_j
    grid = (n_i + 1, n_j)
    bpb = rows_per_batch // tm

    def slab(i, j):
        return (jnp.minimum(i * n_j + j, n_slabs - 1), 0)

    slab_spec = pl.BlockSpec((NORM_SLAB, D), slab)
    vec_spec = pl.BlockSpec((1, D), lambda i, j: (0, 0))
    in_specs = [
        slab_spec, slab_spec, vec_spec,
        pl.BlockSpec((None, 2, N_MOD, D),
                     lambda i, j: (jnp.minimum(i, n_i - 1) // bpb, 0, 0, 0)),
        vec_spec,
        pl.BlockSpec((D, tn), lambda i, j: (0, j)),
    ]
    out_specs = [pl.BlockSpec((tm, tn),
                              lambda i, j: (jnp.maximum(i - 1, 0), jnp.where(i == 0, 0, j))),
                 slab_spec]
    out_shape = [jax.ShapeDtypeStruct((M, N), out_dtype), jax.ShapeDtypeStruct((M, D), F32)]
    operands = [y2, x2, post_g.reshape(1, D), mod_pair, next_g.reshape(1, D), w]
    for src in cast_srcs:
        cast_in_spec, cast_out_spec, cast_shape = _cast_slab_specs(src, grid, n_slabs)
        in_specs.append(cast_in_spec)
        out_specs.append(cast_out_spec)
        out_shape.append(cast_shape)
        operands.append(src[0])
    return pl.pallas_call(
        functools.partial(_norm_matmul_kernel, n_i=n_i, n_cast=len(cast_srcs),
                          gate_idx=gate_idx, next_shift_idx=next_shift_idx,
                          epilogue=epilogue),
        grid=grid,
        in_specs=in_specs,
        out_specs=out_specs,
        out_shape=out_shape,
        scratch_shapes=[pltpu.VMEM((tm, D), BF16), pltpu.VMEM((tm, D), BF16)],
        compiler_params=_params(("arbitrary", "arbitrary")),
        name=name,
    )(*operands)


def _matmul_kernel(*refs, n_lhs, k_sizes, n_k, epilogue, n_cast):
    lhs_refs = refs[:n_lhs]
    w_ref = refs[n_lhs]
    n_in = n_lhs + 1 + n_cast
    o_ref = refs[n_in]
    for t in range(n_cast):
        refs[n_in + 1 + t][...] = refs[n_lhs + 1 + t][...].astype(BF16)
    acc = None
    off = 0
    for a_ref, ks in zip(lhs_refs, k_sizes):
        part = jnp.dot(a_ref[...], w_ref[off:off + ks, :], preferred_element_type=F32)
        acc = part if acc is None else acc + part
        off += ks

    if n_k == 1:
        if epilogue == "relu2":
            acc = jnp.square(jnp.maximum(acc, 0.0))
        o_ref[...] = acc.astype(o_ref.dtype)
    else:
        k = pl.program_id(2)

        @pl.when(k == 0)
        def _():
            o_ref[...] = acc

        @pl.when(k > 0)
        def _():
            o_ref[...] += acc


def _matmul(lhs_list, w, *, out_dtype, tm, tn, tk=None, col_offset=0, n_cols=None,
            epilogue=None, cast_srcs=(), name="matmul"):
    M = lhs_list[0].shape[0]
    K, N_total = w.shape
    N = N_total - col_offset if n_cols is None else n_cols
    k_sizes = tuple(a.shape[1] for a in lhs_list)
    assert sum(k_sizes) == K and M % tm == 0 and N % tn == 0 and col_offset % tn == 0
    joff = col_offset // tn
    if tk is None:
        n_k = 1
        in_specs = [pl.BlockSpec((tm, ks), lambda i, j: (i, 0)) for ks in k_sizes]
        in_specs.append(pl.BlockSpec((K, tn), lambda i, j: (0, j + joff)))
        out_specs = [pl.BlockSpec((tm, tn), lambda i, j: (i, j))]
        grid = (M // tm, N // tn)
        kern_k_sizes = k_sizes
    else:
        assert len(lhs_list) == 1 and K % tk == 0 and out_dtype == F32 and epilogue is None
        n_k = K // tk
        in_specs = [pl.BlockSpec((tm, tk), lambda i, j, k: (i, k)),
                    pl.BlockSpec((tk, tn), lambda i, j, k: (k, j + joff))]
        out_specs = [pl.BlockSpec((tm, tn), lambda i, j, k: (i, j))]
        grid = (M // tm, N // tn, n_k)
        kern_k_sizes = (tk,)
    out_shape = [jax.ShapeDtypeStruct((M, N), out_dtype)]
    operands = list(lhs_list) + [w]
    for src in cast_srcs:
        cast_in_spec, cast_out_spec, cast_shape = _cast_slab_specs(src, grid)
        in_specs.append(cast_in_spec)
        out_specs.append(cast_out_spec)
        out_shape.append(cast_shape)
        operands.append(src[0])
    res = pl.pallas_call(
        functools.partial(_matmul_kernel, n_lhs=len(lhs_list), k_sizes=kern_k_sizes,
                          n_k=n_k, epilogue=epilogue, n_cast=len(cast_srcs)),
        grid=grid,
        in_specs=in_specs,
        out_specs=out_specs,
        out_shape=out_shape,
        compiler_params=_params(("arbitrary",) * len(grid)),
        name=name,
    )(*operands)
    return res if cast_srcs else res[0]


def _bucket_thresholds():
    max_exact = N_BUCKETS // 2
    ths = []
    for b in range(max_exact + 1, N_BUCKETS):
        n = max_exact
        while True:
            v = max_exact + int(math.log(n / max_exact) / math.log(MAX_DISTANCE / max_exact)
                                * (N_BUCKETS - max_exact))
            if min(v, N_BUCKETS - 1) >= b:
                break
            n += 1
        ths.append(n)
    return ths


def _bias_kernel(rb_ref, o_ref):
    h = pl.program_id(0)
    T, C = ATTN_BLOCK, MAX_DISTANCE
    row = lax.broadcasted_iota(jnp.int32, (C, C), 0)
    col = lax.broadcasted_iota(jnp.int32, (C, C), 1)
    far = rb_ref[N_BUCKETS - 1, h]

    def band_tile(offset):
        d = row - col + offset
        n = jnp.maximum(d, 0)
        bucket = jnp.minimum(n, N_BUCKETS // 2)
        for th in _bucket_thresholds():
            bucket = bucket + (n >= th).astype(jnp.int32)
        tile = jnp.zeros((C, C), F32)
        for b in range(N_BUCKETS - 1):
            tile = jnp.where(bucket == b, (rb_ref[b, h] - far) * LOG2E, tile)
        return jnp.where(d >= 0, tile, -jnp.inf)

    on_diag = band_tile(0)
    below_diag = band_tile(C)
    zeros = jnp.zeros((C, C), F32)
    future = jnp.full((C, C), -jnp.inf, F32)
    nb = T // C
    for i in range(nb):
        for j in range(nb):
            sub = on_diag if i == j else below_diag if i == j + 1 else zeros if i > j else future
            o_ref[0, 0, i * C:(i + 1) * C, j * C:(j + 1) * C] = sub
            near = i == 0 and j == nb - 1
            o_ref[0, 1, i * C:(i + 1) * C, j * C:(j + 1) * C] = below_diag if near else zeros


def _bias_tiles(rel_bias):
    T = ATTN_BLOCK
    return pl.pallas_call(
        _bias_kernel,
        grid=(N_HEADS,),
        in_specs=[pl.BlockSpec(memory_space=pltpu.SMEM)],
        out_specs=pl.BlockSpec((1, 2, T, T), lambda h: (h, 0, 0, 0)),
        out_shape=jax.ShapeDtypeStruct((N_HEADS, 2, T, T), F32),
        compiler_params=_params(("arbitrary",)),
        name="bias_tiles",
    )(rel_bias)


def _lane_tile(x, n):
    return x if n == 1 else jnp.concatenate([x] * n, axis=1)


def _attn_kernel(q_ref, k_ref, v_ref, bias_ref, lam_ref, g_ref, cast_in_ref, o_ref,
                 cast_out_ref, m_ref, l_ref, acc_ref, sa_ref, sb_ref, *, lambda_init):
    T = ATTN_BLOCK
    qi = pl.program_id(2)
    scale2 = HEAD_DIM ** -0.5 * LOG2E

    cast_out_ref[...] = cast_in_ref[...].astype(BF16)
    m_ref[...] = jnp.full(m_ref.shape, -jnp.inf, F32)
    l_ref[...] = jnp.zeros(l_ref.shape, F32)
    acc_ref[...] = jnp.zeros(acc_ref.shape, F32)

    def scores(j, s_ref):
        start = pl.multiple_of(j * T, T)
        k = k_ref[pl.ds(start, T), :]
        for c in range(2):
            lo, hi = c * HEAD_DIM, (c + 1) * HEAD_DIM
            s_ref[c] = lax.dot_general(
                q_ref[:, lo:hi], k[:, lo:hi], (((1,), (1,)), ((), ())),
                preferred_element_type=F32) * scale2

    def consume(j, s_ref, bias):
        start = pl.multiple_of(j * T, T)
        v = v_ref[pl.ds(start, T), :]
        for c in range(2):
            s = s_ref[c]
            if bias is not None:
                s = s + bias
            m_old = m_ref[c]
            m_new = jnp.maximum(m_old, jnp.max(s, axis=-1, keepdims=True))
            alpha = jnp.exp2(m_old - m_new)
            p = jnp.exp2(s - _lane_tile(m_new, T // LANES))
            p_part = p[:, :LANES]
            for t in range(1, T // LANES):
                p_part = p_part + p[:, t * LANES:(t + 1) * LANES]
            l_ref[c] = alpha * l_ref[c] + p_part
            acc_ref[c] = (_lane_tile(alpha, V_DIM // LANES) * acc_ref[c]
                          + jnp.dot(p.astype(BF16), v, preferred_element_type=F32))
            m_ref[c] = m_new

    scores(0, sa_ref)

    def plain_pair(i, carry):
        j = 2 * i
        scores(j + 1, sb_ref)
        consume(j, sa_ref, None)
        scores(j + 2, sa_ref)
        consume(j + 1, sb_ref, None)
        return carry

    n_plain = jnp.maximum(qi - 1, 0)
    lax.fori_loop(0, n_plain // 2, plain_pair, 0)

    @pl.when(qi == 0)
    def _():
        consume(0, sa_ref, bias_ref[0, 0])

    @pl.when(qi % 2 == 1)
    def _():
        scores(qi, sb_ref)
        consume(qi - 1, sa_ref, bias_ref[0, 1])
        consume(qi, sb_ref, bias_ref[0, 0])

    @pl.when(jnp.logical_and(qi % 2 == 0, qi >= 2))
    def _():
        scores(qi - 1, sb_ref)
        consume(qi - 2, sa_ref, None)
        scores(qi, sa_ref)
        consume(qi - 1, sb_ref, bias_ref[0, 1])
        consume(qi, sa_ref, bias_ref[0, 0])

    lam_v = lam_ref[...]
    lam = (jnp.exp(jnp.sum(lam_v[0:1] * lam_v[1:2], axis=-1, keepdims=True))
           - jnp.exp(jnp.sum(lam_v[2:3] * lam_v[3:4], axis=-1, keepdims=True))
           + lambda_init)
    l0 = jnp.sum(l_ref[0], axis=-1, keepdims=True)
    l1 = jnp.sum(l_ref[1], axis=-1, keepdims=True)
    out = acc_ref[0] / l0 - lam * (acc_ref[1] / l1)
    o_ref[...] = (_rms(out, g_ref[...]) * (1.0 - lambda_init)).astype(o_ref.dtype)


def _attention(qkv, bias_tiles, lam_vecs, subln_g, lambda_init, B, S, cast_src):
    T = ATTN_BLOCK
    nq = S // T
    grid = (B, N_HEADS, nq)
    cast_in_spec, cast_out_spec, cast_shape = _cast_slab_specs(cast_src, grid)
    return pl.pallas_call(
        functools.partial(_attn_kernel, lambda_init=lambda_init),
        grid=grid,
        in_specs=[
            pl.BlockSpec((T, V_DIM), lambda b, h, i: (b * nq + i, h)),
            pl.BlockSpec((S, V_DIM), lambda b, h, i: (b, N_HEADS + h)),
            pl.BlockSpec((S, V_DIM), lambda b, h, i: (b, 2 * N_HEADS + h)),
            pl.BlockSpec((1, 2, T, T), lambda b, h, i: (h, 0, 0, 0)),
            pl.BlockSpec((4, HEAD_DIM), lambda b, h, i: (0, 0)),
            pl.BlockSpec((1, V_DIM), lambda b, h, i: (0, 0)),
            cast_in_spec,
        ],
        out_specs=[pl.BlockSpec((T, V_DIM), lambda b, h, i: (b * nq + i, h)),
                   cast_out_spec],
        out_shape=[jax.ShapeDtypeStruct((B * S, ATTN_WIDTH), BF16), cast_shape],
        scratch_shapes=[pltpu.VMEM((2, T, LANES), F32), pltpu.VMEM((2, T, LANES), F32),
                        pltpu.VMEM((2, T, V_DIM), F32), pltpu.VMEM((2, T, T), F32),
                        pltpu.VMEM((2, T, T), F32)],
        compiler_params=_params(("arbitrary", "arbitrary", "arbitrary")),
        name="diff_attention",
    )(qkv, qkv, qkv, bias_tiles, lam_vecs, subln_g.reshape(1, V_DIM), cast_src[0])


def _gate_kernel(z_ref, ws_ref, bs_ref, vg_ref, vb_ref, o_ref):
    z = z_ref[...]
    a = 0.5 * z * (1.0 + lax.erf(z * math.sqrt(0.5)))
    u = a[:, :GMLP_WIDTH]
    v = a[:, GMLP_WIDTH:]
    mu = jnp.mean(v, axis=-1, keepdims=True)
    vc = v - mu
    var = jnp.mean(vc * vc, axis=-1, keepdims=True)
    vn = (vc * lax.rsqrt(var + NORM_EPS) * vg_ref[...] + vb_ref[...]).astype(BF16)
    row = lax.broadcasted_iota(jnp.int32, (CHUNK, CHUNK), 0)
    col = lax.broadcasted_iota(jnp.int32, (CHUNK, CHUNK), 1)
    causal = row >= col
    bs = bs_ref[...]
    for g in range(N_GROUPS):
        ws = jnp.where(causal, ws_ref[g], 0.0).astype(BF16)
        cols = slice(g * GROUP_DIM, (g + 1) * GROUP_DIM)
        for n in range(z.shape[0] // CHUNK):
            rows = slice(n * CHUNK, (n + 1) * CHUNK)
            mixed = jnp.dot(ws, vn[rows, cols], preferred_element_type=F32) + bs[:, g:g + 1]
            o_ref[rows, cols] = (u[rows, cols] * mixed).astype(o_ref.dtype)


def _spatial_gate(z2, w_s, b_s, v_g, v_b):
    M = z2.shape[0]
    R = ROW_BLOCK
    return pl.pallas_call(
        _gate_kernel,
        grid=(M // R,),
        in_specs=[
            pl.BlockSpec((R, 2 * GMLP_WIDTH), lambda i: (i, 0)),
            pl.BlockSpec((N_GROUPS, CHUNK, CHUNK), lambda i: (0, 0, 0)),
            pl.BlockSpec((CHUNK, N_GROUPS), lambda i: (0, 0)),
            pl.BlockSpec((1, GMLP_WIDTH), lambda i: (0, 0)),
            pl.BlockSpec((1, GMLP_WIDTH), lambda i: (0, 0)),
        ],
        out_specs=pl.BlockSpec((R, GMLP_WIDTH), lambda i: (i, 0)),
        out_shape=jax.ShapeDtypeStruct((M, GMLP_WIDTH), BF16),
        compiler_params=_params(("arbitrary",)),
        name="spatial_gate",
    )(z2, w_s, b_s.T, v_g.reshape(1, GMLP_WIDTH), v_b.reshape(1, GMLP_WIDTH))


def kernel(x, c, rel_bias, w_ada, b_ada, pre_mix_g, w_in, lambda_q1, lambda_k1,
           lambda_q2, lambda_k2, subln_g, v_norm_g, v_norm_b, w_s, b_s, w_out,
           post_mix_g, pre_mlp_g, w_1, w_2, post_mlp_g):
    B, S, D = x.shape
    assert (D, x.dtype) == (D_MODEL, F32) and S % ATTN_BLOCK == 0 and S % ROW_BLOCK == 0
    M = B * S
    mod = _ada_mod(c, w_ada, b_ada)
    bias_tiles = _bias_tiles(rel_bias)
    x2 = x.reshape(M, D)
    h, w_in_b = _prenorm(x2, pre_mix_g[0], mod[0], 0, S, cast_src=(w_in, 0))
    for l in range(DEPTH):
        lambda_init = 0.8 - 0.6 * math.exp(-0.3 * l)
        qkv = _matmul([h], w_in_b, out_dtype=BF16, tm=1024, tn=1024, n_cols=QKV_COLS,
                      name="in_proj_qkv")
        z, w_out_b = _matmul([h], w_in_b, out_dtype=F32, tm=1024, tn=1024,
                             col_offset=QKV_COLS, cast_srcs=((w_out, l),), name="in_proj_z")
        lam_vecs = jnp.stack([lambda_q1[l], lambda_k1[l], lambda_q2[l], lambda_k2[l]])
        attn, w_1_b = _attention(qkv, bias_tiles, lam_vecs, subln_g[l], lambda_init, B, S,
                                 cast_src=(w_1, l))
        gm = _spatial_gate(z, w_s[l], b_s[l], v_norm_g[l], v_norm_b[l])
        y = _matmul([attn, gm], w_out_b, out_dtype=F32, tm=1024, tn=1024, name="out_proj")
        mod_same = jnp.stack([mod[l], mod[l]], axis=1)
        a, x2, w_2_b = _norm_matmul(y, x2, post_mix_g[l], mod_same, pre_mlp_g[l], 2, 3, S,
                                    w_1_b, out_dtype=BF16, tm=1024, tn=1024,
                                    epilogue="relu2", cast_srcs=((w_2, l),), name="mlp_up")
        if l + 1 < DEPTH:
            y, w_in_b = _matmul([a], w_2_b, out_dtype=F32, tm=1024, tn=1024, tk=4096,
                                cast_srcs=((w_in, l + 1),), name="mlp_down")
            mod_next = jnp.stack([mod[l], mod[l + 1]], axis=1)
            x2, h = _postnorm(y, x2, post_mlp_g[l], mod_next, pre_mix_g[l + 1], 5, 0, S)
        else:
            y = _matmul([a], w_2_b, out_dtype=F32, tm=1024, tn=1024, tk=4096,
                        name="mlp_down")
            x2, _ = _postnorm(y, x2, post_mlp_g[l], mod_same, post_mlp_g[l], 5, None, S)
    return x2.reshape(B, S, D)
```

```python
import functools
import math

import jax
import jax.numpy as jnp
from jax import lax
from jax.experimental import pallas as pl
from jax.experimental.pallas import tpu as pltpu

F32 = jnp.float32
BF16 = jnp.bfloat16

D_MODEL = 4096
DEPTH = 2
ATTN_WIDTH = 2048
GMLP_WIDTH = 2048
HEAD_DIM = 128
V_DIM = 2 * HEAD_DIM
N_HEADS = ATTN_WIDTH // V_DIM
N_GROUPS = 8
GROUP_DIM = GMLP_WIDTH // N_GROUPS
CHUNK = 128
D_FF = 4 * D_MODEL
N_BUCKETS = 32
MAX_DISTANCE = 128
NORM_EPS = 1e-6
N_MOD = 6
QKV_COLS = 3 * ATTN_WIDTH
IN_COLS = QKV_COLS + 2 * GMLP_WIDTH

V7X_VMEM_BYTES = 64 * 1024 * 1024
VMEM_LIMIT_BYTES = V7X_VMEM_BYTES - 4 * 1024 * 1024
SUBLANES = 8
BF16_SUBLANES = 16
LANES = 128
LOG2E = math.log2(math.e)

ATTN_BLOCK = 512
ROW_BLOCK = 256
GATE_BLOCK = 512
NORM_SLAB = 64


def _params(semantics):
    return pltpu.CompilerParams(dimension_semantics=semantics,
                                vmem_limit_bytes=VMEM_LIMIT_BYTES)


def _ada_kernel(c_ref, w_ref, b_ref, o_ref):
    c = c_ref[...]
    c_act = (c * jax.nn.sigmoid(c)).astype(BF16)
    w = w_ref[0].astype(BF16)
    o_ref[0] = jnp.dot(c_act, w, preferred_element_type=F32) + b_ref[0]


def _ada_mod(c, w_ada, b_ada):
    B, D = c.shape
    L, _, N = w_ada.shape
    tn = 512
    c_pad = jnp.zeros((SUBLANES, D), F32).at[:B].set(c)
    out = pl.pallas_call(
        _ada_kernel,
        grid=(L, N // tn),
        in_specs=[
            pl.BlockSpec((SUBLANES, D), lambda l, j: (0, 0)),
            pl.BlockSpec((1, D, tn), lambda l, j: (l, 0, j)),
            pl.BlockSpec((1, 1, tn), lambda l, j: (l, 0, j)),
        ],
        out_specs=pl.BlockSpec((1, SUBLANES, tn), lambda l, j: (l, 0, j)),
        out_shape=jax.ShapeDtypeStruct((L, SUBLANES, N), F32),
        compiler_params=_params(("arbitrary", "arbitrary")),
        name="ada_mod",
    )(c_pad, w_ada, b_ada.reshape(L, 1, N))
    return out[:, :B].reshape(L, B, N_MOD, D)


def _cast_slab_specs(cast_src, grid, n_slabs=None):
    stacked, layer = cast_src
    n_steps = math.prod(grid)
    n_slabs = n_steps if n_slabs is None else n_slabs
    _, rows, cols = stacked.shape
    assert rows % (n_slabs * BF16_SUBLANES) == 0 and n_slabs <= n_steps
    slab = rows // n_slabs
    strides = [math.prod(grid[d + 1:]) for d in range(len(grid))]

    def step(*g):
        t = sum(gi * st for gi, st in zip(g, strides))
        return t if n_slabs == n_steps else jnp.minimum(t, n_slabs - 1)

    return (pl.BlockSpec((None, slab, cols), lambda *g: (layer, step(*g), 0)),
            pl.BlockSpec((slab, cols), lambda *g: (step(*g), 0)),
            jax.ShapeDtypeStruct((rows, cols), BF16))


def _rms(x, g):
    ms = jnp.mean(x * x, axis=-1, keepdims=True)
    return x * lax.rsqrt(ms + NORM_EPS) * g


def _prenorm_kernel(x_ref, g_ref, mod_ref, cast_in_ref, h_ref, cast_out_ref, *, shift_idx):
    cast_out_ref[...] = cast_in_ref[...].astype(BF16)
    shift = mod_ref[0, shift_idx:shift_idx + 1, :]
    scale = mod_ref[0, shift_idx + 1:shift_idx + 2, :]
    h = _rms(x_ref[...], g_ref[...]) * (1.0 + scale) + shift
    h_ref[...] = h.astype(BF16)


def _prenorm(x2, g, mod, shift_idx, rows_per_batch, cast_src):
    M, D = x2.shape
    tm = ROW_BLOCK
    bpb = rows_per_batch // tm
    grid = (M // tm,)
    cast_in_spec, cast_out_spec, cast_shape = _cast_slab_specs(cast_src, grid)
    return pl.pallas_call(
        functools.partial(_prenorm_kernel, shift_idx=shift_idx),
        grid=grid,
        in_specs=[
            pl.BlockSpec((tm, D), lambda i: (i, 0)),
            pl.BlockSpec((1, D), lambda i: (0, 0)),
            pl.BlockSpec((1, N_MOD, D), lambda i: (i // bpb, 0, 0)),
            cast_in_spec,
        ],
        out_specs=[pl.BlockSpec((tm, D), lambda i: (i, 0)), cast_out_spec],
        out_shape=[jax.ShapeDtypeStruct((M, D), BF16), cast_shape],
        compiler_params=_params(("arbitrary",)),
        name="prenorm",
    )(x2, g.reshape(1, D), mod, cast_src[0])


def _postnorm_kernel(y_ref, x_ref, pg_ref, mod_ref, ng_ref, xo_ref, h_ref=None, *,
                     gate_idx, next_shift_idx):
    gate = mod_ref[0, gate_idx:gate_idx + 1, :]
    x_new = x_ref[...] + gate * _rms(y_ref[...], pg_ref[...])
    xo_ref[...] = x_new
    if next_shift_idx is not None:
        shift = mod_ref[1, next_shift_idx:next_shift_idx + 1, :]
        scale = mod_ref[1, next_shift_idx + 1:next_shift_idx + 2, :]
        h = _rms(x_new, ng_ref[...]) * (1.0 + scale) + shift
        h_ref[...] = h.astype(BF16)


def _postnorm(y2, x2, post_g, mod_pair, next_g, gate_idx, next_shift_idx, rows_per_batch):
    M, D = x2.shape
    tm = ROW_BLOCK
    bpb = rows_per_batch // tm
    row_spec = pl.BlockSpec((tm, D), lambda i: (i, 0))
    vec_spec = pl.BlockSpec((1, D), lambda i: (0, 0))
    with_next = next_shift_idx is not None
    kern = functools.partial(_postnorm_kernel, gate_idx=gate_idx,
                             next_shift_idx=next_shift_idx)
    out_shape = [jax.ShapeDtypeStruct((M, D), F32)]
    out_specs = [row_spec]
    if with_next:
        out_shape.append(jax.ShapeDtypeStruct((M, D), BF16))
        out_specs.append(row_spec)
    res = pl.pallas_call(
        kern,
        grid=(M // tm,),
        in_specs=[
            row_spec, row_spec, vec_spec,
            pl.BlockSpec((None, 2, N_MOD, D), lambda i: (i // bpb, 0, 0, 0)),
            vec_spec,
        ],
        out_specs=out_specs,
        out_shape=out_shape,
        compiler_params=_params(("arbitrary",)),
        name="postnorm",
    )(y2, x2, post_g.reshape(1, D), mod_pair, next_g.reshape(1, D))
    return (res[0], res[1]) if with_next else (res[0], None)


def _norm_matmul_kernel(y_ref, x_ref, pg_ref, mod_ref, ng_ref, w_ref, *rest, n_i, n_cast,
                        gate_idx, next_shift_idx, epilogue):
    cast_in = rest[:n_cast]
    o_ref, xo_ref = rest[n_cast], rest[n_cast + 1]
    cast_out = rest[n_cast + 2:2 * n_cast + 2]
    h_refs = rest[2 * n_cast + 2:]
    i = pl.program_id(0)
    j = pl.program_id(1)
    def norm_slab(h_ref):
        for src, dst in zip(cast_in, cast_out):
            dst[...] = src[...].astype(BF16)
        gate = mod_ref[0, gate_idx:gate_idx + 1, :]
        shift = mod_ref[1, next_shift_idx:next_shift_idx + 1, :]
        scale = mod_ref[1, next_shift_idx + 1:next_shift_idx + 2, :]
        x_new = x_ref[...] + gate * _rms(y_ref[...], pg_ref[...])
        xo_ref[...] = x_new
        h = _rms(x_new, ng_ref[...]) * (1.0 + scale) + shift
        rows = pl.ds(pl.multiple_of(j * NORM_SLAB, NORM_SLAB), NORM_SLAB)
        h_ref[rows, :] = h.astype(BF16)

    def matmul(h_ref):
        r = jnp.dot(h_ref[...], w_ref[...], preferred_element_type=F32)
        if epilogue == "relu2":
            r = jnp.square(jnp.maximum(r, 0.0))
        o_ref[...] = r.astype(o_ref.dtype)

    @pl.when(i == 0)
    def _():
        norm_slab(h_refs[0])

    for parity in (0, 1):
        @pl.when(jnp.logical_and(jnp.logical_and(i >= 1, i < n_i), i % 2 == parity))
        def _():
            norm_slab(h_refs[parity])
            matmul(h_refs[1 - parity])

    @pl.when(i == n_i)
    def _():
        matmul(h_refs[(n_i - 1) % 2])


def _norm_matmul(y2, x2, post_g, mod_pair, next_g, gate_idx, next_shift_idx, rows_per_batch,
                 w, *, out_dtype, tm, tn, epilogue=None, cast_srcs=(), name="norm_matmul"):
    M, D = x2.shape
    N = w.shape[1]
    n_i, n_j = M // tm, N // tn
    assert M % tm == 0 and N % tn == 0 and tm == n_j * NORM_SLAB
    n_slabs = n_i * n_j
    grid = (n_i + 1, n_j)
    bpb = rows_per_batch // tm

    def slab(i, j):
        return (jnp.minimum(i * n_j + j, n_slabs - 1), 0)

    def col(i, j):
        return jnp.where(i == 0, 0, j)

    slab_spec = pl.BlockSpec((NORM_SLAB, D), slab)
    vec_spec = pl.BlockSpec((1, D), lambda i, j: (0, 0))
    in_specs = [
        slab_spec, slab_spec, vec_spec,
        pl.BlockSpec((None, 2, N_MOD, D),
                     lambda i, j: (jnp.minimum(i, n_i - 1) // bpb, 0, 0, 0)),
        vec_spec,
        pl.BlockSpec((D, tn), lambda i, j: (0, col(i, j))),
    ]
    out_specs = [pl.BlockSpec((tm, tn), lambda i, j: (jnp.maximum(i - 1, 0), col(i, j))),
                 slab_spec]
    out_shape = [jax.ShapeDtypeStruct((M, N), out_dtype), jax.ShapeDtypeStruct((M, D), F32)]
    operands = [y2, x2, post_g.reshape(1, D), mod_pair, next_g.reshape(1, D), w]
    for src in cast_srcs:
        cast_in_spec, cast_out_spec, cast_shape = _cast_slab_specs(src, grid, n_slabs)
        in_specs.append(cast_in_spec)
        out_specs.append(cast_out_spec)
        out_shape.append(cast_shape)
        operands.append(src[0])
    return pl.pallas_call(
        functools.partial(_norm_matmul_kernel, n_i=n_i, n_cast=len(cast_srcs),
                          gate_idx=gate_idx, next_shift_idx=next_shift_idx,
                          epilogue=epilogue),
        grid=grid,
        in_specs=in_specs,
        out_specs=out_specs,
        out_shape=out_shape,
        scratch_shapes=[pltpu.VMEM((tm, D), BF16), pltpu.VMEM((tm, D), BF16)],
        compiler_params=_params(("arbitrary", "arbitrary")),
        name=name,
    )(*operands)


def _matmul_kernel(*refs, n_lhs, k_sizes, n_k, epilogue, n_cast):
    lhs_refs = refs[:n_lhs]
    w_ref = refs[n_lhs]
    n_in = n_lhs + 1 + n_cast
    o_ref = refs[n_in]
    for t in range(n_cast):
        refs[n_in + 1 + t][...] = refs[n_lhs + 1 + t][...].astype(BF16)
    acc = None
    off = 0
    for a_ref, ks in zip(lhs_refs, k_sizes):
        part = jnp.dot(a_ref[...], w_ref[off:off + ks, :], preferred_element_type=F32)
        acc = part if acc is None else acc + part
        off += ks

    if n_k == 1:
        if epilogue == "relu2":
            acc = jnp.square(jnp.maximum(acc, 0.0))
        o_ref[...] = acc.astype(o_ref.dtype)
    else:
        k = pl.program_id(2)

        @pl.when(k == 0)
        def _():
            o_ref[...] = acc

        @pl.when(k > 0)
        def _():
            o_ref[...] += acc


def _matmul(lhs_list, w, *, out_dtype, tm, tn, tk=None, col_offset=0, n_cols=None,
            epilogue=None, cast_srcs=(), name="matmul"):
    M = lhs_list[0].shape[0]
    K, N_total = w.shape
    N = N_total - col_offset if n_cols is None else n_cols
    k_sizes = tuple(a.shape[1] for a in lhs_list)
    assert sum(k_sizes) == K and M % tm == 0 and N % tn == 0 and col_offset % tn == 0
    joff = col_offset // tn
    if tk is None:
        n_k = 1
        in_specs = [pl.BlockSpec((tm, ks), lambda i, j: (i, 0)) for ks in k_sizes]
        in_specs.append(pl.BlockSpec((K, tn), lambda i, j: (0, j + joff)))
        out_specs = [pl.BlockSpec((tm, tn), lambda i, j: (i, j))]
        grid = (M // tm, N // tn)
        kern_k_sizes = k_sizes
    else:
        assert len(lhs_list) == 1 and K % tk == 0 and out_dtype == F32 and epilogue is None
        n_k = K // tk
        in_specs = [pl.BlockSpec((tm, tk), lambda i, j, k: (i, k)),
                    pl.BlockSpec((tk, tn), lambda i, j, k: (k, j + joff))]
        out_specs = [pl.BlockSpec((tm, tn), lambda i, j, k: (i, j))]
        grid = (M // tm, N // tn, n_k)
        kern_k_sizes = (tk,)
    out_shape = [jax.ShapeDtypeStruct((M, N), out_dtype)]
    operands = list(lhs_list) + [w]
    for src in cast_srcs:
        cast_in_spec, cast_out_spec, cast_shape = _cast_slab_specs(src, grid)
        in_specs.append(cast_in_spec)
        out_specs.append(cast_out_spec)
        out_shape.append(cast_shape)
        operands.append(src[0])
    res = pl.pallas_call(
        functools.partial(_matmul_kernel, n_lhs=len(lhs_list), k_sizes=kern_k_sizes,
                          n_k=n_k, epilogue=epilogue, n_cast=len(cast_srcs)),
        grid=grid,
        in_specs=in_specs,
        out_specs=out_specs,
        out_shape=out_shape,
        compiler_params=_params(("arbitrary",) * len(grid)),
        name=name,
    )(*operands)
    return res if cast_srcs else res[0]


def _bucket_thresholds():
    max_exact = N_BUCKETS // 2
    ths = []
    for b in range(max_exact + 1, N_BUCKETS):
        n = max_exact
        while True:
            v = max_exact + int(math.log(n / max_exact) / math.log(MAX_DISTANCE / max_exact)
                                * (N_BUCKETS - max_exact))
            if min(v, N_BUCKETS - 1) >= b:
                break
            n += 1
        ths.append(n)
    return ths


def _bias_kernel(rb_ref, o_ref):
    h = pl.program_id(0)
    T, C = ATTN_BLOCK, MAX_DISTANCE
    row = lax.broadcasted_iota(jnp.int32, (C, C), 0)
    col = lax.broadcasted_iota(jnp.int32, (C, C), 1)
    far = rb_ref[N_BUCKETS - 1, h]

    def band_tile(offset):
        d = row - col + offset
        n = jnp.maximum(d, 0)
        bucket = jnp.minimum(n, N_BUCKETS // 2)
        for th in _bucket_thresholds():
            bucket = bucket + (n >= th).astype(jnp.int32)
        tile = jnp.zeros((C, C), F32)
        for b in range(N_BUCKETS - 1):
            tile = jnp.where(bucket == b, (rb_ref[b, h] - far) * LOG2E, tile)
        return jnp.where(d >= 0, tile, -jnp.inf)

    on_diag = band_tile(0)
    below_diag = band_tile(C)
    zeros = jnp.zeros((C, C), F32)
    future = jnp.full((C, C), -jnp.inf, F32)
    nb = T // C
    for i in range(nb):
        for j in range(nb):
            sub = on_diag if i == j else below_diag if i == j + 1 else zeros if i > j else future
            o_ref[0, 0, i * C:(i + 1) * C, j * C:(j + 1) * C] = sub
            near = i == 0 and j == nb - 1
            o_ref[0, 1, i * C:(i + 1) * C, j * C:(j + 1) * C] = below_diag if near else zeros


def _bias_tiles(rel_bias):
    T = ATTN_BLOCK
    return pl.pallas_call(
        _bias_kernel,
        grid=(N_HEADS,),
        in_specs=[pl.BlockSpec(memory_space=pltpu.SMEM)],
        out_specs=pl.BlockSpec((1, 2, T, T), lambda h: (h, 0, 0, 0)),
        out_shape=jax.ShapeDtypeStruct((N_HEADS, 2, T, T), F32),
        compiler_params=_params(("arbitrary",)),
        name="bias_tiles",
    )(rel_bias)


def _lane_tile(x, n):
    return x if n == 1 else jnp.concatenate([x] * n, axis=1)


def _attn_kernel(q_ref, k_ref, v_ref, bias_ref, lam_ref, g_ref, cast_in_ref, o_ref,
                 cast_out_ref, m_ref, l_ref, acc_ref, sa_ref, sb_ref, *, lambda_init):
    T = ATTN_BLOCK
    qi = pl.program_id(2)
    scale2 = HEAD_DIM ** -0.5 * LOG2E

    cast_out_ref[...] = cast_in_ref[...].astype(BF16)
    m_ref[...] = jnp.full(m_ref.shape, -jnp.inf, F32)
    l_ref[...] = jnp.zeros(l_ref.shape, F32)
    acc_ref[...] = jnp.zeros(acc_ref.shape, F32)

    def scores(j, s_ref):
        start = pl.multiple_of(j * T, T)
        k = k_ref[pl.ds(start, T), :]
        for c in range(2):
            lo, hi = c * HEAD_DIM, (c + 1) * HEAD_DIM
            s_ref[c] = lax.dot_general(
                q_ref[:, lo:hi], k[:, lo:hi], (((1,), (1,)), ((), ())),
                preferred_element_type=F32) * scale2

    def consume(j, s_ref, bias):
        start = pl.multiple_of(j * T, T)
        v = v_ref[pl.ds(start, T), :]
        for c in range(2):
            s = s_ref[c]
            if bias is not None:
                s = s + bias
            m_old = m_ref[c]
            m_new = jnp.maximum(m_old, jnp.max(s, axis=-1, keepdims=True))
            alpha = jnp.exp2(m_old - m_new)
            p = jnp.exp2(s - _lane_tile(m_new, T // LANES))
            p_part = p[:, :LANES]
            for t in range(1, T // LANES):
                p_part = p_part + p[:, t * LANES:(t + 1) * LANES]
            l_ref[c] = alpha * l_ref[c] + p_part
            acc_ref[c] = (_lane_tile(alpha, V_DIM // LANES) * acc_ref[c]
                          + jnp.dot(p.astype(BF16), v, preferred_element_type=F32))
            m_ref[c] = m_new

    scores(0, sa_ref)

    def plain_pair(i, carry):
        j = 2 * i
        scores(j + 1, sb_ref)
        consume(j, sa_ref, None)
        scores(j + 2, sa_ref)
        consume(j + 1, sb_ref, None)
        return carry

    n_plain = jnp.maximum(qi - 1, 0)
    lax.fori_loop(0, n_plain // 2, plain_pair, 0)

    @pl.when(qi == 0)
    def _():
        consume(0, sa_ref, bias_ref[0, 0])

    @pl.when(qi % 2 == 1)
    def _():
        scores(qi, sb_ref)
        consume(qi - 1, sa_ref, bias_ref[0, 1])
        consume(qi, sb_ref, bias_ref[0, 0])

    @pl.when(jnp.logical_and(qi % 2 == 0, qi >= 2))
    def _():
        scores(qi - 1, sb_ref)
        consume(qi - 2, sa_ref, None)
        scores(qi, sa_ref)
        consume(qi - 1, sb_ref, bias_ref[0, 1])
        consume(qi, sa_ref, bias_ref[0, 0])

    lam_v = lam_ref[...]
    lam = (jnp.exp(jnp.sum(lam_v[0:1] * lam_v[1:2], axis=-1, keepdims=True))
           - jnp.exp(jnp.sum(lam_v[2:3] * lam_v[3:4], axis=-1, keepdims=True))
           + lambda_init)
    l0 = jnp.sum(l_ref[0], axis=-1, keepdims=True)
    l1 = jnp.sum(l_ref[1], axis=-1, keepdims=True)
    out = acc_ref[0] / l0 - lam * (acc_ref[1] / l1)
    o_ref[...] = (_rms(out, g_ref[...]) * (1.0 - lambda_init)).astype(o_ref.dtype)


def _attention(qkv, bias_tiles, lam_vecs, subln_g, lambda_init, B, S, cast_src):
    T = ATTN_BLOCK
    nq = S // T
    grid = (B, N_HEADS, nq)
    cast_in_spec, cast_out_spec, cast_shape = _cast_slab_specs(cast_src, grid)
    return pl.pallas_call(
        functools.partial(_attn_kernel, lambda_init=lambda_init),
        grid=grid,
        in_specs=[
            pl.BlockSpec((T, V_DIM), lambda b, h, i: (b * nq + i, h)),
            pl.BlockSpec((S, V_DIM), lambda b, h, i: (b, N_HEADS + h)),
            pl.BlockSpec((S, V_DIM), lambda b, h, i: (b, 2 * N_HEADS + h)),
            pl.BlockSpec((1, 2, T, T), lambda b, h, i: (h, 0, 0, 0)),
            pl.BlockSpec((4, HEAD_DIM), lambda b, h, i: (0, 0)),
            pl.BlockSpec((1, V_DIM), lambda b, h, i: (0, 0)),
            cast_in_spec,
        ],
        out_specs=[pl.BlockSpec((T, V_DIM), lambda b, h, i: (b * nq + i, h)),
                   cast_out_spec],
        out_shape=[jax.ShapeDtypeStruct((B * S, ATTN_WIDTH), BF16), cast_shape],
        scratch_shapes=[pltpu.VMEM((2, T, LANES), F32), pltpu.VMEM((2, T, LANES), F32),
                        pltpu.VMEM((2, T, V_DIM), F32), pltpu.VMEM((2, T, T), F32),
                        pltpu.VMEM((2, T, T), F32)],
        compiler_params=_params(("arbitrary", "arbitrary", "arbitrary")),
        name="diff_attention",
    )(qkv, qkv, qkv, bias_tiles, lam_vecs, subln_g.reshape(1, V_DIM), cast_src[0])


def _gate_kernel(z_ref, ws_ref, bs_ref, vg_ref, vb_ref, o_ref):
    z = z_ref[...]
    a = 0.5 * z * (1.0 + lax.erf(z * math.sqrt(0.5)))
    u = a[:, :GMLP_WIDTH]
    v = a[:, GMLP_WIDTH:]
    mu = jnp.mean(v, axis=-1, keepdims=True)
    vc = v - mu
    var = jnp.mean(vc * vc, axis=-1, keepdims=True)
    vn = (vc * lax.rsqrt(var + NORM_EPS) * vg_ref[...] + vb_ref[...]).astype(BF16)
    row = lax.broadcasted_iota(jnp.int32, (CHUNK, CHUNK), 0)
    col = lax.broadcasted_iota(jnp.int32, (CHUNK, CHUNK), 1)
    causal = row >= col
    bs = bs_ref[...]
    for g in range(N_GROUPS):
        ws = jnp.where(causal, ws_ref[g], 0.0).astype(BF16)
        cols = slice(g * GROUP_DIM, (g + 1) * GROUP_DIM)
        for n in range(z.shape[0] // CHUNK):
            rows = slice(n * CHUNK, (n + 1) * CHUNK)
            mixed = jnp.dot(ws, vn[rows, cols], preferred_element_type=F32) + bs[:, g:g + 1]
            o_ref[rows, cols] = (u[rows, cols] * mixed).astype(o_ref.dtype)


def _spatial_gate(z2, w_s, b_s, v_g, v_b):
    M = z2.shape[0]
    R = GATE_BLOCK
    return pl.pallas_call(
        _gate_kernel,
        grid=(M // R,),
        in_specs=[
            pl.BlockSpec((R, 2 * GMLP_WIDTH), lambda i: (i, 0)),
            pl.BlockSpec((N_GROUPS, CHUNK, CHUNK), lambda i: (0, 0, 0)),
            pl.BlockSpec((CHUNK, N_GROUPS), lambda i: (0, 0)),
            pl.BlockSpec((1, GMLP_WIDTH), lambda i: (0, 0)),
            pl.BlockSpec((1, GMLP_WIDTH), lambda i: (0, 0)),
        ],
        out_specs=pl.BlockSpec((R, GMLP_WIDTH), lambda i: (i, 0)),
        out_shape=jax.ShapeDtypeStruct((M, GMLP_WIDTH), BF16),
        compiler_params=_params(("arbitrary",)),
        name="spatial_gate",
    )(z2, w_s, b_s.T, v_g.reshape(1, GMLP_WIDTH), v_b.reshape(1, GMLP_WIDTH))


def kernel(x, c, rel_bias, w_ada, b_ada, pre_mix_g, w_in, lambda_q1, lambda_k1,
           lambda_q2, lambda_k2, subln_g, v_norm_g, v_norm_b, w_s, b_s, w_out,
           post_mix_g, pre_mlp_g, w_1, w_2, post_mlp_g):
    B, S, D = x.shape
    assert (D, x.dtype) == (D_MODEL, F32) and S % ATTN_BLOCK == 0 and S % ROW_BLOCK == 0
    M = B * S
    mod = _ada_mod(c, w_ada, b_ada)
    bias_tiles = _bias_tiles(rel_bias)
    x2 = x.reshape(M, D)
    h, w_in_b = _prenorm(x2, pre_mix_g[0], mod[0], 0, S, cast_src=(w_in, 0))
    for l in range(DEPTH):
        lambda_init = 0.8 - 0.6 * math.exp(-0.3 * l)
        qkv = _matmul([h], w_in_b, out_dtype=BF16, tm=1024, tn=1024, n_cols=QKV_COLS,
                      name="in_proj_qkv")
        z, w_out_b = _matmul([h], w_in_b, out_dtype=F32, tm=1024, tn=1024,
                             col_offset=QKV_COLS, cast_srcs=((w_out, l),), name="in_proj_z")
        lam_vecs = jnp.stack([lambda_q1[l], lambda_k1[l], lambda_q2[l], lambda_k2[l]])
        attn, w_1_b = _attention(qkv, bias_tiles, lam_vecs, subln_g[l], lambda_init, B, S,
                                 cast_src=(w_1, l))
        gm = _spatial_gate(z, w_s[l], b_s[l], v_norm_g[l], v_norm_b[l])
        y = _matmul([attn, gm], w_out_b, out_dtype=F32, tm=1024, tn=1024, name="out_proj")
        mod_same = jnp.stack([mod[l], mod[l]], axis=1)
        a, x2, w_2_b = _norm_matmul(y, x2, post_mix_g[l], mod_same, pre_mlp_g[l], 2, 3, S,
                                    w_1_b, out_dtype=BF16, tm=1024, tn=1024,
                                    epilogue="relu2", cast_srcs=((w_2, l),), name="mlp_up")
        if l + 1 < DEPTH:
            y, w_in_b = _matmul([a], w_2_b, out_dtype=F32, tm=1024, tn=1024, tk=4096,
                                cast_srcs=((w_in, l + 1),), name="mlp_down")
            mod_next = jnp.stack([mod[l], mod[l + 1]], axis=1)
            x2, h = _postnorm(y, x2, post_mlp_g[l], mod_next, pre_mix_g[l + 1], 5, 0, S)
        else:
            y = _matmul([a], w_2_b, out_dtype=F32, tm=1024, tn=1024, tk=4096,
                        name="mlp_down")
            x2, _ = _postnorm(y, x2, post_mlp_g[l], mod_same, post_mlp_g[l], 5, None, S)
    return x2.reshape(B, S, D)
```

```python
import functools
import math

import jax
import jax.numpy as jnp
from jax import lax
from jax.experimental import pallas as pl
from jax.experimental.pallas import tpu as pltpu

F32 = jnp.float32
BF16 = jnp.bfloat16

D_MODEL = 4096
DEPTH = 2
ATTN_WIDTH = 2048
GMLP_WIDTH = 2048
HEAD_DIM = 128
V_DIM = 2 * HEAD_DIM
N_HEADS = ATTN_WIDTH // V_DIM
N_GROUPS = 8
GROUP_DIM = GMLP_WIDTH // N_GROUPS
CHUNK = 128
N_BUCKETS = 32
MAX_DISTANCE = 128
NORM_EPS = 1e-6
N_MOD = 6
QKV_COLS = 3 * ATTN_WIDTH

V7X_VMEM_BYTES = 64 * 1024 * 1024
VMEM_LIMIT_BYTES = V7X_VMEM_BYTES - 4 * 1024 * 1024
SUBLANES = 8
BF16_SUBLANES = 16
LANES = 128
LOG2E = math.log2(math.e)

MATMUL_TILE = 1024
MATMUL_K_TILE = 4096
ADA_TILE = 1024
ATTN_BLOCK = 512
ROW_BLOCK = 256
GATE_BLOCK = 512
NORM_SLAB = 64


def _params(semantics):
    return pltpu.CompilerParams(dimension_semantics=semantics,
                                vmem_limit_bytes=VMEM_LIMIT_BYTES)


def _ada_kernel(c_ref, w_ref, b_ref, o_ref):
    c = c_ref[...]
    c_act = (c * jax.nn.sigmoid(c)).astype(BF16)
    w = w_ref[0].astype(BF16)
    o_ref[0] = jnp.dot(c_act, w, preferred_element_type=F32) + b_ref[0]


def _ada_mod(c, w_ada, b_ada):
    B, D = c.shape
    L, _, N = w_ada.shape
    tn = ADA_TILE
    c_pad = jnp.zeros((SUBLANES, D), F32).at[:B].set(c)
    out = pl.pallas_call(
        _ada_kernel,
        grid=(L, N // tn),
        in_specs=[
            pl.BlockSpec((SUBLANES, D), lambda l, j: (0, 0)),
            pl.BlockSpec((1, D, tn), lambda l, j: (l, 0, j)),
            pl.BlockSpec((1, 1, tn), lambda l, j: (l, 0, j)),
        ],
        out_specs=pl.BlockSpec((1, SUBLANES, tn), lambda l, j: (l, 0, j)),
        out_shape=jax.ShapeDtypeStruct((L, SUBLANES, N), F32),
        compiler_params=_params(("arbitrary", "arbitrary")),
        name="ada_mod",
    )(c_pad, w_ada, b_ada.reshape(L, 1, N))
    return out[:, :B].reshape(L, B, N_MOD, D)


def _cast_slab_specs(cast_src, grid, n_slabs=None):
    stacked, layer = cast_src
    n_steps = math.prod(grid)
    n_slabs = n_steps if n_slabs is None else n_slabs
    _, rows, cols = stacked.shape
    assert rows % (n_slabs * BF16_SUBLANES) == 0 and n_slabs <= n_steps
    slab = rows // n_slabs
    strides = [math.prod(grid[d + 1:]) for d in range(len(grid))]

    def step(*g):
        t = sum(gi * st for gi, st in zip(g, strides))
        return t if n_slabs == n_steps else jnp.minimum(t, n_slabs - 1)

    return (pl.BlockSpec((None, slab, cols), lambda *g: (layer, step(*g), 0)),
            pl.BlockSpec((slab, cols), lambda *g: (step(*g), 0)),
            jax.ShapeDtypeStruct((rows, cols), BF16))


def _rms(x, g):
    ms = jnp.mean(x * x, axis=-1, keepdims=True)
    return x * lax.rsqrt(ms + NORM_EPS) * g


def _prenorm_kernel(x_ref, g_ref, mod_ref, cast_in_ref, h_ref, cast_out_ref, *, shift_idx):
    cast_out_ref[...] = cast_in_ref[...].astype(BF16)
    shift = mod_ref[0, shift_idx:shift_idx + 1, :]
    scale = mod_ref[0, shift_idx + 1:shift_idx + 2, :]
    h = _rms(x_ref[...], g_ref[...]) * (1.0 + scale) + shift
    h_ref[...] = h.astype(BF16)


def _prenorm(x2, g, mod, shift_idx, rows_per_batch, cast_src):
    M, D = x2.shape
    tm = ROW_BLOCK
    bpb = rows_per_batch // tm
    grid = (M // tm,)
    cast_in_spec, cast_out_spec, cast_shape = _cast_slab_specs(cast_src, grid)
    return pl.pallas_call(
        functools.partial(_prenorm_kernel, shift_idx=shift_idx),
        grid=grid,
        in_specs=[
            pl.BlockSpec((tm, D), lambda i: (i, 0)),
            pl.BlockSpec((1, D), lambda i: (0, 0)),
            pl.BlockSpec((1, N_MOD, D), lambda i: (i // bpb, 0, 0)),
            cast_in_spec,
        ],
        out_specs=[pl.BlockSpec((tm, D), lambda i: (i, 0)), cast_out_spec],
        out_shape=[jax.ShapeDtypeStruct((M, D), BF16), cast_shape],
        compiler_params=_params(("arbitrary",)),
        name="prenorm",
    )(x2, g.reshape(1, D), mod, cast_src[0])


def _postnorm_kernel(y_ref, x_ref, pg_ref, mod_ref, ng_ref, xo_ref, h_ref=None, *,
                     gate_idx, next_shift_idx):
    gate = mod_ref[0, gate_idx:gate_idx + 1, :]
    x_new = x_ref[...] + gate * _rms(y_ref[...], pg_ref[...])
    xo_ref[...] = x_new
    if next_shift_idx is not None:
        shift = mod_ref[1, next_shift_idx:next_shift_idx + 1, :]
        scale = mod_ref[1, next_shift_idx + 1:next_shift_idx + 2, :]
        h = _rms(x_new, ng_ref[...]) * (1.0 + scale) + shift
        h_ref[...] = h.astype(BF16)


def _postnorm(y2, x2, post_g, mod_pair, next_g, gate_idx, next_shift_idx, rows_per_batch):
    M, D = x2.shape
    tm = ROW_BLOCK
    bpb = rows_per_batch // tm
    row_spec = pl.BlockSpec((tm, D), lambda i: (i, 0))
    vec_spec = pl.BlockSpec((1, D), lambda i: (0, 0))
    with_next = next_shift_idx is not None
    kern = functools.partial(_postnorm_kernel, gate_idx=gate_idx,
                             next_shift_idx=next_shift_idx)
    out_shape = [jax.ShapeDtypeStruct((M, D), F32)]
    out_specs = [row_spec]
    if with_next:
        out_shape.append(jax.ShapeDtypeStruct((M, D), BF16))
        out_specs.append(row_spec)
    res = pl.pallas_call(
        kern,
        grid=(M // tm,),
        in_specs=[
            row_spec, row_spec, vec_spec,
            pl.BlockSpec((None, 2, N_MOD, D), lambda i: (i // bpb, 0, 0, 0)),
            vec_spec,
        ],
        out_specs=out_specs,
        out_shape=out_shape,
        compiler_params=_params(("arbitrary",)),
        name="postnorm",
    )(y2, x2, post_g.reshape(1, D), mod_pair, next_g.reshape(1, D))
    return (res[0], res[1]) if with_next else (res[0], None)


def _norm_matmul_kernel(y_ref, x_ref, pg_ref, mod_ref, ng_ref, w_ref, *rest, n_i, n_cast,
                        gate_idx, next_shift_idx, epilogue):
    cast_in = rest[:n_cast]
    o_ref, xo_ref = rest[n_cast], rest[n_cast + 1]
    cast_out = rest[n_cast + 2:2 * n_cast + 2]
    h_refs = rest[2 * n_cast + 2:]
    i = pl.program_id(0)
    j = pl.program_id(1)
    def norm_slab(h_ref):
        for src, dst in zip(cast_in, cast_out):
            dst[...] = src[...].astype(BF16)
        gate = mod_ref[0, gate_idx:gate_idx + 1, :]
        shift = mod_ref[1, next_shift_idx:next_shift_idx + 1, :]
        scale = mod_ref[1, next_shift_idx + 1:next_shift_idx + 2, :]
        x_new = x_ref[...] + gate * _rms(y_ref[...], pg_ref[...])
        xo_ref[...] = x_new
        h = _rms(x_new, ng_ref[...]) * (1.0 + scale) + shift
        rows = pl.ds(pl.multiple_of(j * NORM_SLAB, NORM_SLAB), NORM_SLAB)
        h_ref[rows, :] = h.astype(BF16)

    def matmul(h_ref):
        r = jnp.dot(h_ref[...], w_ref[...], preferred_element_type=F32)
        if epilogue == "relu2":
            r = jnp.square(jnp.maximum(r, 0.0))
        o_ref[...] = r.astype(o_ref.dtype)

    @pl.when(i == 0)
    def _():
        norm_slab(h_refs[0])

    for parity in (0, 1):
        @pl.when(jnp.logical_and(jnp.logical_and(i >= 1, i < n_i), i % 2 == parity))
        def _():
            norm_slab(h_refs[parity])
            matmul(h_refs[1 - parity])

    @pl.when(i == n_i)
    def _():
        matmul(h_refs[(n_i - 1) % 2])


def _norm_matmul(y2, x2, post_g, mod_pair, next_g, gate_idx, next_shift_idx, rows_per_batch,
                 w, *, out_dtype, epilogue=None, cast_srcs=(), name="norm_matmul"):
    M, D = x2.shape
    N = w.shape[1]
    tm = tn = MATMUL_TILE
    n_i, n_j = M // tm, N // tn
    assert M % tm == 0 and N % tn == 0 and tm == n_j * NORM_SLAB
    n_slabs = n_i * n_j
    grid = (n_i + 1, n_j)
    bpb = rows_per_batch // tm

    def slab(i, j):
        return (jnp.minimum(i * n_j + j, n_slabs - 1), 0)

    def col(i, j):
        return jnp.where(i == 0, 0, j)

    slab_spec = pl.BlockSpec((NORM_SLAB, D), slab)
    vec_spec = pl.BlockSpec((1, D), lambda i, j: (0, 0))
    in_specs = [
        slab_spec, slab_spec, vec_spec,
        pl.BlockSpec((None, 2, N_MOD, D),
                     lambda i, j: (jnp.minimum(i, n_i - 1) // bpb, 0, 0, 0)),
        vec_spec,
        pl.BlockSpec((D, tn), lambda i, j: (0, col(i, j))),
    ]
    out_specs = [pl.BlockSpec((tm, tn), lambda i, j: (jnp.maximum(i - 1, 0), col(i, j))),
                 slab_spec]
    out_shape = [jax.ShapeDtypeStruct((M, N), out_dtype), jax.ShapeDtypeStruct((M, D), F32)]
    operands = [y2, x2, post_g.reshape(1, D), mod_pair, next_g.reshape(1, D), w]
    for src in cast_srcs:
        cast_in_spec, cast_out_spec, cast_shape = _cast_slab_specs(src, grid, n_slabs)
        in_specs.append(cast_in_spec)
        out_specs.append(cast_out_spec)
        out_shape.append(cast_shape)
        operands.append(src[0])
    return pl.pallas_call(
        functools.partial(_norm_matmul_kernel, n_i=n_i, n_cast=len(cast_srcs),
                          gate_idx=gate_idx, next_shift_idx=next_shift_idx,
                          epilogue=epilogue),
        grid=grid,
        in_specs=in_specs,
        out_specs=out_specs,
        out_shape=out_shape,
        scratch_shapes=[pltpu.VMEM((tm, D), BF16), pltpu.VMEM((tm, D), BF16)],
        compiler_params=_params(("arbitrary", "arbitrary")),
        name=name,
    )(*operands)


def _matmul_kernel(*refs, n_lhs, k_sizes, n_k, epilogue, n_cast):
    lhs_refs = refs[:n_lhs]
    w_ref = refs[n_lhs]
    n_in = n_lhs + 1 + n_cast
    o_ref = refs[n_in]
    for t in range(n_cast):
        refs[n_in + 1 + t][...] = refs[n_lhs + 1 + t][...].astype(BF16)
    acc = None
    off = 0
    for a_ref, ks in zip(lhs_refs, k_sizes):
        part = jnp.dot(a_ref[...], w_ref[off:off + ks, :], preferred_element_type=F32)
        acc = part if acc is None else acc + part
        off += ks

    if n_k == 1:
        if epilogue == "relu2":
            acc = jnp.square(jnp.maximum(acc, 0.0))
        o_ref[...] = acc.astype(o_ref.dtype)
    else:
        k = pl.program_id(2)

        @pl.when(k == 0)
        def _():
            o_ref[...] = acc

        @pl.when(k > 0)
        def _():
            o_ref[...] += acc


def _matmul(lhs_list, w, *, out_dtype, tk=None, col_offset=0, n_cols=None,
            epilogue=None, cast_srcs=(), name="matmul"):
    M = lhs_list[0].shape[0]
    K, N_total = w.shape
    N = N_total - col_offset if n_cols is None else n_cols
    tm = tn = MATMUL_TILE
    k_sizes = tuple(a.shape[1] for a in lhs_list)
    assert sum(k_sizes) == K and M % tm == 0 and N % tn == 0 and col_offset % tn == 0
    joff = col_offset // tn
    if tk is None:
        n_k = 1
        in_specs = [pl.BlockSpec((tm, ks), lambda i, j: (i, 0)) for ks in k_sizes]
        in_specs.append(pl.BlockSpec((K, tn), lambda i, j: (0, j + joff)))
        out_specs = [pl.BlockSpec((tm, tn), lambda i, j: (i, j))]
        grid = (M // tm, N // tn)
        kern_k_sizes = k_sizes
    else:
        assert len(lhs_list) == 1 and K % tk == 0 and out_dtype == F32 and epilogue is None
        n_k = K // tk
        in_specs = [pl.BlockSpec((tm, tk), lambda i, j, k: (i, k)),
                    pl.BlockSpec((tk, tn), lambda i, j, k: (k, j + joff))]
        out_specs = [pl.BlockSpec((tm, tn), lambda i, j, k: (i, j))]
        grid = (M // tm, N // tn, n_k)
        kern_k_sizes = (tk,)
    out_shape = [jax.ShapeDtypeStruct((M, N), out_dtype)]
    operands = list(lhs_list) + [w]
    for src in cast_srcs:
        cast_in_spec, cast_out_spec, cast_shape = _cast_slab_specs(src, grid)
        in_specs.append(cast_in_spec)
        out_specs.append(cast_out_spec)
        out_shape.append(cast_shape)
        operands.append(src[0])
    res = pl.pallas_call(
        functools.partial(_matmul_kernel, n_lhs=len(lhs_list), k_sizes=kern_k_sizes,
                          n_k=n_k, epilogue=epilogue, n_cast=len(cast_srcs)),
        grid=grid,
        in_specs=in_specs,
        out_specs=out_specs,
        out_shape=out_shape,
        compiler_params=_params(("arbitrary",) * len(grid)),
        name=name,
    )(*operands)
    return res if cast_srcs else res[0]


def _bucket_thresholds():
    max_exact = N_BUCKETS // 2
    ths = []
    for b in range(max_exact + 1, N_BUCKETS):
        n = max_exact
        while True:
            v = max_exact + int(math.log(n / max_exact) / math.log(MAX_DISTANCE / max_exact)
                                * (N_BUCKETS - max_exact))
            if min(v, N_BUCKETS - 1) >= b:
                break
            n += 1
        ths.append(n)
    return ths


def _bias_kernel(rb_ref, o_ref):
    h = pl.program_id(0)
    T, C = ATTN_BLOCK, MAX_DISTANCE
    row = lax.broadcasted_iota(jnp.int32, (C, C), 0)
    col = lax.broadcasted_iota(jnp.int32, (C, C), 1)
    far = rb_ref[N_BUCKETS - 1, h]

    def band_tile(offset):
        d = row - col + offset
        n = jnp.maximum(d, 0)
        bucket = jnp.minimum(n, N_BUCKETS // 2)
        for th in _bucket_thresholds():
            bucket = bucket + (n >= th).astype(jnp.int32)
        tile = jnp.zeros((C, C), F32)
        for b in range(N_BUCKETS - 1):
            tile = jnp.where(bucket == b, (rb_ref[b, h] - far) * LOG2E, tile)
        return jnp.where(d >= 0, tile, -jnp.inf)

    on_diag = band_tile(0)
    below_diag = band_tile(C)
    zeros = jnp.zeros((C, C), F32)
    future = jnp.full((C, C), -jnp.inf, F32)
    nb = T // C
    for i in range(nb):
        for j in range(nb):
            sub = on_diag if i == j else below_diag if i == j + 1 else zeros if i > j else future
            o_ref[0, 0, i * C:(i + 1) * C, j * C:(j + 1) * C] = sub
            near = i == 0 and j == nb - 1
            o_ref[0, 1, i * C:(i + 1) * C, j * C:(j + 1) * C] = below_diag if near else zeros


def _bias_tiles(rel_bias):
    T = ATTN_BLOCK
    return pl.pallas_call(
        _bias_kernel,
        grid=(N_HEADS,),
        in_specs=[pl.BlockSpec(memory_space=pltpu.SMEM)],
        out_specs=pl.BlockSpec((1, 2, T, T), lambda h: (h, 0, 0, 0)),
        out_shape=jax.ShapeDtypeStruct((N_HEADS, 2, T, T), F32),
        compiler_params=_params(("arbitrary",)),
        name="bias_tiles",
    )(rel_bias)


def _lane_tile(x, n):
    return x if n == 1 else jnp.concatenate([x] * n, axis=1)


def _attn_kernel(q_ref, k_ref, v_ref, bias_ref, lam_ref, g_ref, cast_in_ref, o_ref,
                 cast_out_ref, m_ref, l_ref, acc_ref, sa_ref, sb_ref, *, lambda_init):
    T = ATTN_BLOCK
    qi = pl.program_id(2)
    scale2 = HEAD_DIM ** -0.5 * LOG2E

    cast_out_ref[...] = cast_in_ref[...].astype(BF16)
    m_ref[...] = jnp.full(m_ref.shape, -jnp.inf, F32)
    l_ref[...] = jnp.zeros(l_ref.shape, F32)
    acc_ref[...] = jnp.zeros(acc_ref.shape, F32)

    def scores(j, s_ref):
        start = pl.multiple_of(j * T, T)
        k = k_ref[pl.ds(start, T), :]
        for c in range(2):
            lo, hi = c * HEAD_DIM, (c + 1) * HEAD_DIM
            s_ref[c] = lax.dot_general(
                q_ref[:, lo:hi], k[:, lo:hi], (((1,), (1,)), ((), ())),
                preferred_element_type=F32) * scale2

    def consume(j, s_ref, bias):
        start = pl.multiple_of(j * T, T)
        v = v_ref[pl.ds(start, T), :]
        for c in range(2):
            s = s_ref[c]
            if bias is not None:
                s = s + bias
            m_old = m_ref[c]
            m_new = jnp.maximum(m_old, jnp.max(s, axis=-1, keepdims=True))
            alpha = jnp.exp2(m_old - m_new)
            p = jnp.exp2(s - _lane_tile(m_new, T // LANES))
            p_part = p[:, :LANES]
            for t in range(1, T // LANES):
                p_part = p_part + p[:, t * LANES:(t + 1) * LANES]
            l_ref[c] = alpha * l_ref[c] + p_part
            acc_ref[c] = (_lane_tile(alpha, V_DIM // LANES) * acc_ref[c]
                          + jnp.dot(p.astype(BF16), v, preferred_element_type=F32))
            m_ref[c] = m_new

    scores(0, sa_ref)

    def plain_pair(i, carry):
        j = 2 * i
        scores(j + 1, sb_ref)
        consume(j, sa_ref, None)
        scores(j + 2, sa_ref)
        consume(j + 1, sb_ref, None)
        return carry

    n_plain = jnp.maximum(qi - 1, 0)
    lax.fori_loop(0, n_plain // 2, plain_pair, 0)

    @pl.when(qi == 0)
    def _():
        consume(0, sa_ref, bias_ref[0, 0])

    @pl.when(qi % 2 == 1)
    def _():
        scores(qi, sb_ref)
        consume(qi - 1, sa_ref, bias_ref[0, 1])
        consume(qi, sb_ref, bias_ref[0, 0])

    @pl.when(jnp.logical_and(qi % 2 == 0, qi >= 2))
    def _():
        scores(qi - 1, sb_ref)
        consume(qi - 2, sa_ref, None)
        scores(qi, sa_ref)
        consume(qi - 1, sb_ref, bias_ref[0, 1])
        consume(qi, sa_ref, bias_ref[0, 0])

    lam_v = lam_ref[...]
    lam = (jnp.exp(jnp.sum(lam_v[0:1] * lam_v[1:2], axis=-1, keepdims=True))
           - jnp.exp(jnp.sum(lam_v[2:3] * lam_v[3:4], axis=-1, keepdims=True))
           + lambda_init)
    l0 = jnp.sum(l_ref[0], axis=-1, keepdims=True)
    l1 = jnp.sum(l_ref[1], axis=-1, keepdims=True)
    out = acc_ref[0] / l0 - lam * (acc_ref[1] / l1)
    o_ref[...] = (_rms(out, g_ref[...]) * (1.0 - lambda_init)).astype(o_ref.dtype)


def _attention(qkv, bias_tiles, lam_vecs, subln_g, lambda_init, B, S, cast_src):
    T = ATTN_BLOCK
    nq = S // T
    grid = (B, N_HEADS, nq)
    cast_in_spec, cast_out_spec, cast_shape = _cast_slab_specs(cast_src, grid)
    return pl.pallas_call(
        functools.partial(_attn_kernel, lambda_init=lambda_init),
        grid=grid,
        in_specs=[
            pl.BlockSpec((T, V_DIM), lambda b, h, i: (b * nq + i, h)),
            pl.BlockSpec((S, V_DIM), lambda b, h, i: (b, N_HEADS + h)),
            pl.BlockSpec((S, V_DIM), lambda b, h, i: (b, 2 * N_HEADS + h)),
            pl.BlockSpec((1, 2, T, T), lambda b, h, i: (h, 0, 0, 0)),
            pl.BlockSpec((4, HEAD_DIM), lambda b, h, i: (0, 0)),
            pl.BlockSpec((1, V_DIM), lambda b, h, i: (0, 0)),
            cast_in_spec,
        ],
        out_specs=[pl.BlockSpec((T, V_DIM), lambda b, h, i: (b * nq + i, h)),
                   cast_out_spec],
        out_shape=[jax.ShapeDtypeStruct((B * S, ATTN_WIDTH), BF16), cast_shape],
        scratch_shapes=[pltpu.VMEM((2, T, LANES), F32), pltpu.VMEM((2, T, LANES), F32),
                        pltpu.VMEM((2, T, V_DIM), F32), pltpu.VMEM((2, T, T), F32),
                        pltpu.VMEM((2, T, T), F32)],
        compiler_params=_params(("arbitrary", "arbitrary", "arbitrary")),
        name="diff_attention",
    )(qkv, qkv, qkv, bias_tiles, lam_vecs, subln_g.reshape(1, V_DIM), cast_src[0])


def _gate_kernel(z_ref, ws_ref, bs_ref, vg_ref, vb_ref, o_ref):
    z = z_ref[...]
    a = 0.5 * z * (1.0 + lax.erf(z * math.sqrt(0.5)))
    u = a[:, :GMLP_WIDTH]
    v = a[:, GMLP_WIDTH:]
    mu = jnp.mean(v, axis=-1, keepdims=True)
    vc = v - mu
    var = jnp.mean(vc * vc, axis=-1, keepdims=True)
    vn = (vc * lax.rsqrt(var + NORM_EPS) * vg_ref[...] + vb_ref[...]).astype(BF16)
    row = lax.broadcasted_iota(jnp.int32, (CHUNK, CHUNK), 0)
    col = lax.broadcasted_iota(jnp.int32, (CHUNK, CHUNK), 1)
    causal = row >= col
    bs = bs_ref[...]
    for g in range(N_GROUPS):
        ws = jnp.where(causal, ws_ref[g], 0.0).astype(BF16)
        cols = slice(g * GROUP_DIM, (g + 1) * GROUP_DIM)
        for n in range(z.shape[0] // CHUNK):
            rows = slice(n * CHUNK, (n + 1) * CHUNK)
            mixed = jnp.dot(ws, vn[rows, cols], preferred_element_type=F32) + bs[:, g:g + 1]
            o_ref[rows, cols] = (u[rows, cols] * mixed).astype(o_ref.dtype)


def _spatial_gate(z2, w_s, b_s, v_g, v_b):
    M = z2.shape[0]
    R = GATE_BLOCK
    return pl.pallas_call(
        _gate_kernel,
        grid=(M // R,),
        in_specs=[
            pl.BlockSpec((R, 2 * GMLP_WIDTH), lambda i: (i, 0)),
            pl.BlockSpec((N_GROUPS, CHUNK, CHUNK), lambda i: (0, 0, 0)),
            pl.BlockSpec((CHUNK, N_GROUPS), lambda i: (0, 0)),
            pl.BlockSpec((1, GMLP_WIDTH), lambda i: (0, 0)),
            pl.BlockSpec((1, GMLP_WIDTH), lambda i: (0, 0)),
        ],
        out_specs=pl.BlockSpec((R, GMLP_WIDTH), lambda i: (i, 0)),
        out_shape=jax.ShapeDtypeStruct((M, GMLP_WIDTH), BF16),
        compiler_params=_params(("arbitrary",)),
        name="spatial_gate",
    )(z2, w_s, b_s.T, v_g.reshape(1, GMLP_WIDTH), v_b.reshape(1, GMLP_WIDTH))


def kernel(x, c, rel_bias, w_ada, b_ada, pre_mix_g, w_in, lambda_q1, lambda_k1,
           lambda_q2, lambda_k2, subln_g, v_norm_g, v_norm_b, w_s, b_s, w_out,
           post_mix_g, pre_mlp_g, w_1, w_2, post_mlp_g):
    B, S, D = x.shape
    assert (D, x.dtype) == (D_MODEL, F32) and S % ATTN_BLOCK == 0 and S % ROW_BLOCK == 0
    M = B * S
    mod = _ada_mod(c, w_ada, b_ada)
    bias_tiles = _bias_tiles(rel_bias)
    x2 = x.reshape(M, D)
    h, w_in_b = _prenorm(x2, pre_mix_g[0], mod[0], 0, S, cast_src=(w_in, 0))
    for l in range(DEPTH):
        lambda_init = 0.8 - 0.6 * math.exp(-0.3 * l)
        qkv = _matmul([h], w_in_b, out_dtype=BF16, n_cols=QKV_COLS,
                      name="in_proj_qkv")
        z, w_out_b = _matmul([h], w_in_b, out_dtype=F32,
                             col_offset=QKV_COLS, cast_srcs=((w_out, l),), name="in_proj_z")
        lam_vecs = jnp.stack([lambda_q1[l], lambda_k1[l], lambda_q2[l], lambda_k2[l]])
        attn, w_1_b = _attention(qkv, bias_tiles, lam_vecs, subln_g[l], lambda_init, B, S,
                                 cast_src=(w_1, l))
        gm = _spatial_gate(z, w_s[l], b_s[l], v_norm_g[l], v_norm_b[l])
        y = _matmul([attn, gm], w_out_b, out_dtype=F32, name="out_proj")
        mod_same = jnp.stack([mod[l], mod[l]], axis=1)
        a, x2, w_2_b = _norm_matmul(y, x2, post_mix_g[l], mod_same, pre_mlp_g[l], 2, 3, S,
                                    w_1_b, out_dtype=BF16,
                                    epilogue="relu2", cast_srcs=((w_2, l),), name="mlp_up")
        if l + 1 < DEPTH:
            y, w_in_b = _matmul([a], w_2_b, out_dtype=F32, tk=MATMUL_K_TILE,
                                cast_srcs=((w_in, l + 1),), name="mlp_down")
            mod_next = jnp.stack([mod[l], mod[l + 1]], axis=1)
            x2, h = _postnorm(y, x2, post_mlp_g[l], mod_next, pre_mix_g[l + 1], 5, 0, S)
        else:
            y = _matmul([a], w_2_b, out_dtype=F32, tk=MATMUL_K_TILE,
                        name="mlp_down")
            x2, _ = _postnorm(y, x2, post_mlp_g[l], mod_same, post_mlp_g[l], 5, None, S)
    return x2.reshape(B, S, D)
```

```python
import functools
import math

import jax
import jax.numpy as jnp
from jax import lax
from jax.experimental import pallas as pl
from jax.experimental.pallas import tpu as pltpu

F32 = jnp.float32
BF16 = jnp.bfloat16

D_MODEL = 4096
DEPTH = 2
ATTN_WIDTH = 2048
GMLP_WIDTH = 2048
HEAD_DIM = 128
V_DIM = 2 * HEAD_DIM
N_HEADS = ATTN_WIDTH // V_DIM
N_GROUPS = 8
GROUP_DIM = GMLP_WIDTH // N_GROUPS
CHUNK = 128
N_BUCKETS = 32
MAX_DISTANCE = 128
NORM_EPS = 1e-6
N_MOD = 6
QKV_COLS = 3 * ATTN_WIDTH

V7X_VMEM_BYTES = 64 * 1024 * 1024
VMEM_LIMIT_BYTES = V7X_VMEM_BYTES - 4 * 1024 * 1024
SUBLANES = 8
BF16_SUBLANES = 16
LANES = 128
LOG2E = math.log2(math.e)

MATMUL_TILE = 1024
MATMUL_K_TILE = 4096
ADA_TILE = 1024
ADA_HOSTED_TILE = 256
ATTN_BLOCK = 512
ROW_BLOCK = 256
GATE_BLOCK = 512
NORM_SLAB = 64


def _params(semantics):
    return pltpu.CompilerParams(dimension_semantics=semantics,
                                vmem_limit_bytes=VMEM_LIMIT_BYTES)


def _ada_slab(c_ref, w_ref, b_ref):
    c = c_ref[...]
    c_act = (c * jax.nn.sigmoid(c)).astype(BF16)
    return jnp.dot(c_act, w_ref[...].astype(BF16), preferred_element_type=F32) + b_ref[...]


def _ada_kernel(c_ref, w_ref, b_ref, o_ref):
    o_ref[...] = _ada_slab(c_ref, w_ref, b_ref)


def _ada_mod(c_pad, w_ada, b_ada3, layer):
    D = c_pad.shape[1]
    N = w_ada.shape[2]
    tn = ADA_TILE
    return pl.pallas_call(
        _ada_kernel,
        grid=(N // tn,),
        in_specs=[
            pl.BlockSpec((SUBLANES, D), lambda j: (0, 0)),
            pl.BlockSpec((None, D, tn), lambda j: (layer, 0, j)),
            pl.BlockSpec((None, 1, tn), lambda j: (layer, 0, j)),
        ],
        out_specs=pl.BlockSpec((SUBLANES, tn), lambda j: (0, j)),
        out_shape=jax.ShapeDtypeStruct((SUBLANES, N), F32),
        compiler_params=_params(("arbitrary",)),
        name="ada_mod",
    )(c_pad, w_ada, b_ada3)


def _cast_slab_specs(cast_src, grid, n_slabs=None):
    stacked, layer = cast_src
    n_steps = math.prod(grid)
    n_slabs = n_steps if n_slabs is None else n_slabs
    _, rows, cols = stacked.shape
    assert rows % (n_slabs * BF16_SUBLANES) == 0 and n_slabs <= n_steps
    slab = rows // n_slabs
    strides = [math.prod(grid[d + 1:]) for d in range(len(grid))]

    def step(*g):
        t = sum(gi * st for gi, st in zip(g, strides))
        return t if n_slabs == n_steps else jnp.minimum(t, n_slabs - 1)

    return (pl.BlockSpec((None, slab, cols), lambda *g: (layer, step(*g), 0)),
            pl.BlockSpec((slab, cols), lambda *g: (step(*g), 0)),
            jax.ShapeDtypeStruct((rows, cols), BF16))


def _rms(x, g):
    ms = jnp.mean(x * x, axis=-1, keepdims=True)
    return x * lax.rsqrt(ms + NORM_EPS) * g


def _prenorm_kernel(x_ref, g_ref, mod_ref, cast_in_ref, h_ref, cast_out_ref, *, shift_idx):
    cast_out_ref[...] = cast_in_ref[...].astype(BF16)
    shift = mod_ref[0, shift_idx:shift_idx + 1, :]
    scale = mod_ref[0, shift_idx + 1:shift_idx + 2, :]
    h = _rms(x_ref[...], g_ref[...]) * (1.0 + scale) + shift
    h_ref[...] = h.astype(BF16)


def _prenorm(x2, g, mod, shift_idx, rows_per_batch, cast_src):
    M, D = x2.shape
    tm = ROW_BLOCK
    bpb = rows_per_batch // tm
    grid = (M // tm,)
    cast_in_spec, cast_out_spec, cast_shape = _cast_slab_specs(cast_src, grid)
    return pl.pallas_call(
        functools.partial(_prenorm_kernel, shift_idx=shift_idx),
        grid=grid,
        in_specs=[
            pl.BlockSpec((tm, D), lambda i: (i, 0)),
            pl.BlockSpec((1, D), lambda i: (0, 0)),
            pl.BlockSpec((1, N_MOD, D), lambda i: (i // bpb, 0, 0)),
            cast_in_spec,
        ],
        out_specs=[pl.BlockSpec((tm, D), lambda i: (i, 0)), cast_out_spec],
        out_shape=[jax.ShapeDtypeStruct((M, D), BF16), cast_shape],
        compiler_params=_params(("arbitrary",)),
        name="prenorm",
    )(x2, g.reshape(1, D), mod, cast_src[0])


def _postnorm_kernel(y_ref, x_ref, pg_ref, mod_ref, ng_ref, xo_ref, h_ref=None, *,
                     gate_idx, next_shift_idx):
    gate = mod_ref[0, gate_idx:gate_idx + 1, :]
    x_new = x_ref[...] + gate * _rms(y_ref[...], pg_ref[...])
    xo_ref[...] = x_new
    if next_shift_idx is not None:
        shift = mod_ref[1, next_shift_idx:next_shift_idx + 1, :]
        scale = mod_ref[1, next_shift_idx + 1:next_shift_idx + 2, :]
        h = _rms(x_new, ng_ref[...]) * (1.0 + scale) + shift
        h_ref[...] = h.astype(BF16)


def _postnorm(y2, x2, post_g, mod_pair, next_g, gate_idx, next_shift_idx, rows_per_batch):
    M, D = x2.shape
    tm = ROW_BLOCK
    bpb = rows_per_batch // tm
    row_spec = pl.BlockSpec((tm, D), lambda i: (i, 0))
    vec_spec = pl.BlockSpec((1, D), lambda i: (0, 0))
    with_next = next_shift_idx is not None
    kern = functools.partial(_postnorm_kernel, gate_idx=gate_idx,
                             next_shift_idx=next_shift_idx)
    out_shape = [jax.ShapeDtypeStruct((M, D), F32)]
    out_specs = [row_spec]
    if with_next:
        out_shape.append(jax.ShapeDtypeStruct((M, D), BF16))
        out_specs.append(row_spec)
    res = pl.pallas_call(
        kern,
        grid=(M // tm,),
        in_specs=[
            row_spec, row_spec, vec_spec,
            pl.BlockSpec((None, 2, N_MOD, D), lambda i: (i // bpb, 0, 0, 0)),
            vec_spec,
        ],
        out_specs=out_specs,
        out_shape=out_shape,
        compiler_params=_params(("arbitrary",)),
        name="postnorm",
    )(y2, x2, post_g.reshape(1, D), mod_pair, next_g.reshape(1, D))
    return (res[0], res[1]) if with_next else (res[0], None)


def _norm_matmul_kernel(y_ref, x_ref, pg_ref, mod_ref, ng_ref, w_ref, *rest, n_i, n_cast,
                        gate_idx, next_shift_idx, epilogue):
    cast_in = rest[:n_cast]
    o_ref, xo_ref = rest[n_cast], rest[n_cast + 1]
    cast_out = rest[n_cast + 2:2 * n_cast + 2]
    h_refs = rest[2 * n_cast + 2:]
    i = pl.program_id(0)
    j = pl.program_id(1)
    def norm_slab(h_ref):
        for src, dst in zip(cast_in, cast_out):
            dst[...] = src[...].astype(BF16)
        gate = mod_ref[0, gate_idx:gate_idx + 1, :]
        shift = mod_ref[1, next_shift_idx:next_shift_idx + 1, :]
        scale = mod_ref[1, next_shift_idx + 1:next_shift_idx + 2, :]
        x_new = x_ref[...] + gate * _rms(y_ref[...], pg_ref[...])
        xo_ref[...] = x_new
        h = _rms(x_new, ng_ref[...]) * (1.0 + scale) + shift
        rows = pl.ds(pl.multiple_of(j * NORM_SLAB, NORM_SLAB), NORM_SLAB)
        h_ref[rows, :] = h.astype(BF16)

    def matmul(h_ref):
        r = jnp.dot(h_ref[...], w_ref[...], preferred_element_type=F32)
        if epilogue == "relu2":
            r = jnp.square(jnp.maximum(r, 0.0))
        o_ref[...] = r.astype(o_ref.dtype)

    @pl.when(i == 0)
    def _():
        norm_slab(h_refs[0])

    for parity in (0, 1):
        @pl.when(jnp.logical_and(jnp.logical_and(i >= 1, i < n_i), i % 2 == parity))
        def _():
            norm_slab(h_refs[parity])
            matmul(h_refs[1 - parity])

    @pl.when(i == n_i)
    def _():
        matmul(h_refs[(n_i - 1) % 2])


def _norm_matmul(y2, x2, post_g, mod_pair, next_g, gate_idx, next_shift_idx, rows_per_batch,
                 w, *, out_dtype, epilogue=None, cast_srcs=(), name="norm_matmul"):
    M, D = x2.shape
    N = w.shape[1]
    tm = tn = MATMUL_TILE
    n_i, n_j = M // tm, N // tn
    assert M % tm == 0 and N % tn == 0 and tm == n_j * NORM_SLAB
    n_slabs = n_i * n_j
    grid = (n_i + 1, n_j)
    bpb = rows_per_batch // tm

    def slab(i, j):
        return (jnp.minimum(i * n_j + j, n_slabs - 1), 0)

    def col(i, j):
        return jnp.where(i == 0, 0, j)

    slab_spec = pl.BlockSpec((NORM_SLAB, D), slab)
    vec_spec = pl.BlockSpec((1, D), lambda i, j: (0, 0))
    in_specs = [
        slab_spec, slab_spec, vec_spec,
        pl.BlockSpec((None, 2, N_MOD, D),
                     lambda i, j: (jnp.minimum(i, n_i - 1) // bpb, 0, 0, 0)),
        vec_spec,
        pl.BlockSpec((D, tn), lambda i, j: (0, col(i, j))),
    ]
    out_specs = [pl.BlockSpec((tm, tn), lambda i, j: (jnp.maximum(i - 1, 0), col(i, j))),
                 slab_spec]
    out_shape = [jax.ShapeDtypeStruct((M, N), out_dtype), jax.ShapeDtypeStruct((M, D), F32)]
    operands = [y2, x2, post_g.reshape(1, D), mod_pair, next_g.reshape(1, D), w]
    for src in cast_srcs:
        cast_in_spec, cast_out_spec, cast_shape = _cast_slab_specs(src, grid, n_slabs)
        in_specs.append(cast_in_spec)
        out_specs.append(cast_out_spec)
        out_shape.append(cast_shape)
        operands.append(src[0])
    return pl.pallas_call(
        functools.partial(_norm_matmul_kernel, n_i=n_i, n_cast=len(cast_srcs),
                          gate_idx=gate_idx, next_shift_idx=next_shift_idx,
                          epilogue=epilogue),
        grid=grid,
        in_specs=in_specs,
        out_specs=out_specs,
        out_shape=out_shape,
        scratch_shapes=[pltpu.VMEM((tm, D), BF16), pltpu.VMEM((tm, D), BF16)],
        compiler_params=_params(("arbitrary", "arbitrary")),
        name=name,
    )(*operands)


def _matmul_kernel(*refs, n_lhs, k_sizes, n_k, epilogue, n_cast):
    lhs_refs = refs[:n_lhs]
    w_ref = refs[n_lhs]
    n_in = n_lhs + 1 + n_cast
    o_ref = refs[n_in]
    for t in range(n_cast):
        refs[n_in + 1 + t][...] = refs[n_lhs + 1 + t][...].astype(BF16)
    acc = None
    off = 0
    for a_ref, ks in zip(lhs_refs, k_sizes):
        part = jnp.dot(a_ref[...], w_ref[off:off + ks, :], preferred_element_type=F32)
        acc = part if acc is None else acc + part
        off += ks

    if n_k == 1:
        if epilogue == "relu2":
            acc = jnp.square(jnp.maximum(acc, 0.0))
        o_ref[...] = acc.astype(o_ref.dtype)
    else:
        k = pl.program_id(2)

        @pl.when(k == 0)
        def _():
            o_ref[...] = acc

        @pl.when(k > 0)
        def _():
            o_ref[...] += acc


def _matmul(lhs_list, w, *, out_dtype, tk=None, col_offset=0, n_cols=None,
            epilogue=None, cast_srcs=(), name="matmul"):
    M = lhs_list[0].shape[0]
    K, N_total = w.shape
    N = N_total - col_offset if n_cols is None else n_cols
    tm = tn = MATMUL_TILE
    k_sizes = tuple(a.shape[1] for a in lhs_list)
    assert sum(k_sizes) == K and M % tm == 0 and N % tn == 0 and col_offset % tn == 0
    joff = col_offset // tn
    if tk is None:
        n_k = 1
        in_specs = [pl.BlockSpec((tm, ks), lambda i, j: (i, 0)) for ks in k_sizes]
        in_specs.append(pl.BlockSpec((K, tn), lambda i, j: (0, j + joff)))
        out_specs = [pl.BlockSpec((tm, tn), lambda i, j: (i, j))]
        grid = (M // tm, N // tn)
        kern_k_sizes = k_sizes
    else:
        assert len(lhs_list) == 1 and K % tk == 0 and out_dtype == F32 and epilogue is None
        n_k = K // tk
        in_specs = [pl.BlockSpec((tm, tk), lambda i, j, k: (i, k)),
                    pl.BlockSpec((tk, tn), lambda i, j, k: (k, j + joff))]
        out_specs = [pl.BlockSpec((tm, tn), lambda i, j, k: (i, j))]
        grid = (M // tm, N // tn, n_k)
        kern_k_sizes = (tk,)
    out_shape = [jax.ShapeDtypeStruct((M, N), out_dtype)]
    operands = list(lhs_list) + [w]
    for src in cast_srcs:
        cast_in_spec, cast_out_spec, cast_shape = _cast_slab_specs(src, grid)
        in_specs.append(cast_in_spec)
        out_specs.append(cast_out_spec)
        out_shape.append(cast_shape)
        operands.append(src[0])
    res = pl.pallas_call(
        functools.partial(_matmul_kernel, n_lhs=len(lhs_list), k_sizes=kern_k_sizes,
                          n_k=n_k, epilogue=epilogue, n_cast=len(cast_srcs)),
        grid=grid,
        in_specs=in_specs,
        out_specs=out_specs,
        out_shape=out_shape,
        compiler_params=_params(("arbitrary",) * len(grid)),
        name=name,
    )(*operands)
    return res if cast_srcs else res[0]


def _bucket_thresholds():
    max_exact = N_BUCKETS // 2
    ths = []
    for b in range(max_exact + 1, N_BUCKETS):
        n = max_exact
        while True:
            v = max_exact + int(math.log(n / max_exact) / math.log(MAX_DISTANCE / max_exact)
                                * (N_BUCKETS - max_exact))
            if min(v, N_BUCKETS - 1) >= b:
                break
            n += 1
        ths.append(n)
    return ths


def _bias_kernel(rb_ref, o_ref):
    h = pl.program_id(0)
    T, C = ATTN_BLOCK, MAX_DISTANCE
    row = lax.broadcasted_iota(jnp.int32, (C, C), 0)
    col = lax.broadcasted_iota(jnp.int32, (C, C), 1)
    far = rb_ref[N_BUCKETS - 1, h]

    def band_tile(offset):
        d = row - col + offset
        n = jnp.maximum(d, 0)
        bucket = jnp.minimum(n, N_BUCKETS // 2)
        for th in _bucket_thresholds():
            bucket = bucket + (n >= th).astype(jnp.int32)
        tile = jnp.zeros((C, C), F32)
        for b in range(N_BUCKETS - 1):
            tile = jnp.where(bucket == b, (rb_ref[b, h] - far) * LOG2E, tile)
        return jnp.where(d >= 0, tile, -jnp.inf)

    on_diag = band_tile(0)
    below_diag = band_tile(C)
    zeros = jnp.zeros((C, C), F32)
    future = jnp.full((C, C), -jnp.inf, F32)
    nb = T // C
    for i in range(nb):
        for j in range(nb):
            sub = on_diag if i == j else below_diag if i == j + 1 else zeros if i > j else future
            o_ref[0, 0, i * C:(i + 1) * C, j * C:(j + 1) * C] = sub
            near = i == 0 and j == nb - 1
            o_ref[0, 1, i * C:(i + 1) * C, j * C:(j + 1) * C] = below_diag if near else zeros


def _bias_tiles(rel_bias):
    T = ATTN_BLOCK
    return pl.pallas_call(
        _bias_kernel,
        grid=(N_HEADS,),
        in_specs=[pl.BlockSpec(memory_space=pltpu.SMEM)],
        out_specs=pl.BlockSpec((1, 2, T, T), lambda h: (h, 0, 0, 0)),
        out_shape=jax.ShapeDtypeStruct((N_HEADS, 2, T, T), F32),
        compiler_params=_params(("arbitrary",)),
        name="bias_tiles",
    )(rel_bias)


def _lane_tile(x, n):
    return x if n == 1 else jnp.concatenate([x] * n, axis=1)


def _attn_kernel(q_ref, k_ref, v_ref, bias_ref, lam_ref, g_ref, cast_in_ref, *rest,
                 lambda_init, n_ada):
    T = ATTN_BLOCK
    qi = pl.program_id(2)
    scale2 = HEAD_DIM ** -0.5 * LOG2E

    if n_ada:
        c_ref, wada_ref, bada_ref, o_ref, cast_out_ref, mod_ref = rest[:6]
        step = (pl.program_id(0) * pl.num_programs(1) + pl.program_id(1)) * pl.num_programs(2) + qi

        @pl.when(step < n_ada)
        def _():
            mod_ref[...] = _ada_slab(c_ref, wada_ref, bada_ref)
    else:
        o_ref, cast_out_ref = rest[:2]
    m_ref, l_ref, acc_ref, sa_ref, sb_ref = rest[-5:]

    cast_out_ref[...] = cast_in_ref[...].astype(BF16)
    m_ref[...] = jnp.full(m_ref.shape, -jnp.inf, F32)
    l_ref[...] = jnp.zeros(l_ref.shape, F32)
    acc_ref[...] = jnp.zeros(acc_ref.shape, F32)

    def scores(j, s_ref):
        start = pl.multiple_of(j * T, T)
        k = k_ref[pl.ds(start, T), :]
        for c in range(2):
            lo, hi = c * HEAD_DIM, (c + 1) * HEAD_DIM
            s_ref[c] = lax.dot_general(
                q_ref[:, lo:hi], k[:, lo:hi], (((1,), (1,)), ((), ())),
                preferred_element_type=F32) * scale2

    def consume(j, s_ref, bias):
        start = pl.multiple_of(j * T, T)
        v = v_ref[pl.ds(start, T), :]
        for c in range(2):
            s = s_ref[c]
            if bias is not None:
                s = s + bias
            m_old = m_ref[c]
            m_new = jnp.maximum(m_old, jnp.max(s, axis=-1, keepdims=True))
            alpha = jnp.exp2(m_old - m_new)
            p = jnp.exp2(s - _lane_tile(m_new, T // LANES))
            p_part = p[:, :LANES]
            for t in range(1, T // LANES):
                p_part = p_part + p[:, t * LANES:(t + 1) * LANES]
            l_ref[c] = alpha * l_ref[c] + p_part
            acc_ref[c] = (_lane_tile(alpha, V_DIM // LANES) * acc_ref[c]
                          + jnp.dot(p.astype(BF16), v, preferred_element_type=F32))
            m_ref[c] = m_new

    scores(0, sa_ref)

    def plain_pair(i, carry):
        j = 2 * i
        scores(j + 1, sb_ref)
        consume(j, sa_ref, None)
        scores(j + 2, sa_ref)
        consume(j + 1, sb_ref, None)
        return carry

    n_plain = jnp.maximum(qi - 1, 0)
    lax.fori_loop(0, n_plain // 2, plain_pair, 0)

    @pl.when(qi == 0)
    def _():
        consume(0, sa_ref, bias_ref[0, 0])

    @pl.when(qi % 2 == 1)
    def _():
        scores(qi, sb_ref)
        consume(qi - 1, sa_ref, bias_ref[0, 1])
        consume(qi, sb_ref, bias_ref[0, 0])

    @pl.when(jnp.logical_and(qi % 2 == 0, qi >= 2))
    def _():
        scores(qi - 1, sb_ref)
        consume(qi - 2, sa_ref, None)
        scores(qi, sa_ref)
        consume(qi - 1, sb_ref, bias_ref[0, 1])
        consume(qi, sa_ref, bias_ref[0, 0])

    lam_v = lam_ref[...]
    lam = (jnp.exp(jnp.sum(lam_v[0:1] * lam_v[1:2], axis=-1, keepdims=True))
           - jnp.exp(jnp.sum(lam_v[2:3] * lam_v[3:4], axis=-1, keepdims=True))
           + lambda_init)
    l0 = jnp.sum(l_ref[0], axis=-1, keepdims=True)
    l1 = jnp.sum(l_ref[1], axis=-1, keepdims=True)
    out = acc_ref[0] / l0 - lam * (acc_ref[1] / l1)
    o_ref[...] = (_rms(out, g_ref[...]) * (1.0 - lambda_init)).astype(o_ref.dtype)


def _attention(qkv, bias_tiles, lam_vecs, subln_g, lambda_init, B, S, cast_src, ada=None):
    T = ATTN_BLOCK
    nq = S // T
    grid = (B, N_HEADS, nq)
    cast_in_spec, cast_out_spec, cast_shape = _cast_slab_specs(cast_src, grid)
    in_specs = [
        pl.BlockSpec((T, V_DIM), lambda b, h, i: (b * nq + i, h)),
        pl.BlockSpec((S, V_DIM), lambda b, h, i: (b, N_HEADS + h)),
        pl.BlockSpec((S, V_DIM), lambda b, h, i: (b, 2 * N_HEADS + h)),
        pl.BlockSpec((1, 2, T, T), lambda b, h, i: (h, 0, 0, 0)),
        pl.BlockSpec((4, HEAD_DIM), lambda b, h, i: (0, 0)),
        pl.BlockSpec((1, V_DIM), lambda b, h, i: (0, 0)),
        cast_in_spec,
    ]
    out_specs = [pl.BlockSpec((T, V_DIM), lambda b, h, i: (b * nq + i, h)), cast_out_spec]
    out_shape = [jax.ShapeDtypeStruct((B * S, ATTN_WIDTH), BF16), cast_shape]
    operands = [qkv, qkv, qkv, bias_tiles, lam_vecs, subln_g.reshape(1, V_DIM), cast_src[0]]
    n_ada = 0
    if ada is not None:
        c_pad, w_ada, b_ada3, layer = ada
        D, n_mod = w_ada.shape[1:]
        tn = ADA_HOSTED_TILE
        n_ada = n_mod // tn
        assert n_mod % tn == 0 and n_ada <= math.prod(grid)

        def col(b, h, i):
            return jnp.minimum((b * N_HEADS + h) * nq + i, n_ada - 1)

        in_specs += [
            pl.BlockSpec((SUBLANES, D), lambda b, h, i: (0, 0)),
            pl.BlockSpec((None, D, tn), lambda b, h, i: (layer, 0, col(b, h, i))),
            pl.BlockSpec((None, 1, tn), lambda b, h, i: (layer, 0, col(b, h, i))),
        ]
        out_specs.append(pl.BlockSpec((SUBLANES, tn), lambda b, h, i: (0, col(b, h, i))))
        out_shape.append(jax.ShapeDtypeStruct((SUBLANES, n_mod), F32))
        operands += [c_pad, w_ada, b_ada3]
    return pl.pallas_call(
        functools.partial(_attn_kernel, lambda_init=lambda_init, n_ada=n_ada),
        grid=grid,
        in_specs=in_specs,
        out_specs=out_specs,
        out_shape=out_shape,
        scratch_shapes=[pltpu.VMEM((2, T, LANES), F32), pltpu.VMEM((2, T, LANES), F32),
                        pltpu.VMEM((2, T, V_DIM), F32), pltpu.VMEM((2, T, T), F32),
                        pltpu.VMEM((2, T, T), F32)],
        compiler_params=_params(("arbitrary", "arbitrary", "arbitrary")),
        name="diff_attention",
    )(*operands)


def _gate_kernel(z_ref, ws_ref, bs_ref, vg_ref, vb_ref, o_ref):
    z = z_ref[...]
    a = 0.5 * z * (1.0 + lax.erf(z * math.sqrt(0.5)))
    u = a[:, :GMLP_WIDTH]
    v = a[:, GMLP_WIDTH:]
    mu = jnp.mean(v, axis=-1, keepdims=True)
    vc = v - mu
    var = jnp.mean(vc * vc, axis=-1, keepdims=True)
    vn = (vc * lax.rsqrt(var + NORM_EPS) * vg_ref[...] + vb_ref[...]).astype(BF16)
    row = lax.broadcasted_iota(jnp.int32, (CHUNK, CHUNK), 0)
    col = lax.broadcasted_iota(jnp.int32, (CHUNK, CHUNK), 1)
    causal = row >= col
    bs = bs_ref[...]
    for g in range(N_GROUPS):
        ws = jnp.where(causal, ws_ref[g], 0.0).astype(BF16)
        cols = slice(g * GROUP_DIM, (g + 1) * GROUP_DIM)
        for n in range(z.shape[0] // CHUNK):
            rows = slice(n * CHUNK, (n + 1) * CHUNK)
            mixed = jnp.dot(ws, vn[rows, cols], preferred_element_type=F32) + bs[:, g:g + 1]
            o_ref[rows, cols] = (u[rows, cols] * mixed).astype(o_ref.dtype)


def _spatial_gate(z2, w_s, b_s, v_g, v_b):
    M = z2.shape[0]
    R = GATE_BLOCK
    return pl.pallas_call(
        _gate_kernel,
        grid=(M // R,),
        in_specs=[
            pl.BlockSpec((R, 2 * GMLP_WIDTH), lambda i: (i, 0)),
            pl.BlockSpec((N_GROUPS, CHUNK, CHUNK), lambda i: (0, 0, 0)),
            pl.BlockSpec((CHUNK, N_GROUPS), lambda i: (0, 0)),
            pl.BlockSpec((1, GMLP_WIDTH), lambda i: (0, 0)),
            pl.BlockSpec((1, GMLP_WIDTH), lambda i: (0, 0)),
        ],
        out_specs=pl.BlockSpec((R, GMLP_WIDTH), lambda i: (i, 0)),
        out_shape=jax.ShapeDtypeStruct((M, GMLP_WIDTH), BF16),
        compiler_params=_params(("arbitrary",)),
        name="spatial_gate",
    )(z2, w_s, b_s.T, v_g.reshape(1, GMLP_WIDTH), v_b.reshape(1, GMLP_WIDTH))


def kernel(x, c, rel_bias, w_ada, b_ada, pre_mix_g, w_in, lambda_q1, lambda_k1,
           lambda_q2, lambda_k2, subln_g, v_norm_g, v_norm_b, w_s, b_s, w_out,
           post_mix_g, pre_mlp_g, w_1, w_2, post_mlp_g):
    B, S, D = x.shape
    assert (D, x.dtype) == (D_MODEL, F32) and S % ATTN_BLOCK == 0 and S % ROW_BLOCK == 0
    M = B * S
    c_pad = jnp.zeros((SUBLANES, D), F32).at[:B].set(c)
    b_ada3 = b_ada.reshape(DEPTH, 1, N_MOD * D)

    def mod_rows(raw):
        return raw[:B].reshape(B, N_MOD, D)

    mod = mod_rows(_ada_mod(c_pad, w_ada, b_ada3, 0))
    bias_tiles = _bias_tiles(rel_bias)
    x2 = x.reshape(M, D)
    h, w_in_b = _prenorm(x2, pre_mix_g[0], mod, 0, S, cast_src=(w_in, 0))
    for l in range(DEPTH):
        last = l + 1 == DEPTH
        lambda_init = 0.8 - 0.6 * math.exp(-0.3 * l)
        qkv = _matmul([h], w_in_b, out_dtype=BF16, n_cols=QKV_COLS,
                      name="in_proj_qkv")
        z, w_out_b = _matmul([h], w_in_b, out_dtype=F32,
                             col_offset=QKV_COLS, cast_srcs=((w_out, l),), name="in_proj_z")
        lam_vecs = jnp.stack([lambda_q1[l], lambda_k1[l], lambda_q2[l], lambda_k2[l]])
        attn, w_1_b, *mod_next_raw = _attention(
            qkv, bias_tiles, lam_vecs, subln_g[l], lambda_init, B, S, cast_src=(w_1, l),
            ada=None if last else (c_pad, w_ada, b_ada3, l + 1))
        gm = _spatial_gate(z, w_s[l], b_s[l], v_norm_g[l], v_norm_b[l])
        y = _matmul([attn, gm], w_out_b, out_dtype=F32, name="out_proj")
        mod_same = jnp.stack([mod, mod], axis=1)
        a, x2, w_2_b = _norm_matmul(y, x2, post_mix_g[l], mod_same, pre_mlp_g[l], 2, 3, S,
                                    w_1_b, out_dtype=BF16,
                                    epilogue="relu2", cast_srcs=((w_2, l),), name="mlp_up")
        if last:
            y = _matmul([a], w_2_b, out_dtype=F32, tk=MATMUL_K_TILE,
                        name="mlp_down")
            x2, _ = _postnorm(y, x2, post_mlp_g[l], mod_same, post_mlp_g[l], 5, None, S)
        else:
            y, w_in_b = _matmul([a], w_2_b, out_dtype=F32, tk=MATMUL_K_TILE,
                                cast_srcs=((w_in, l + 1),), name="mlp_down")
            mod_next = mod_rows(mod_next_raw[0])
            mod_pair = jnp.stack([mod, mod_next], axis=1)
            x2, h = _postnorm(y, x2, post_mlp_g[l], mod_pair, pre_mix_g[l + 1], 5, 0, S)
            mod = mod_next
    return x2.reshape(B, S, D)
```

```python
import functools
import math

import jax
import jax.numpy as jnp
from jax import lax
from jax.experimental import pallas as pl
from jax.experimental.pallas import tpu as pltpu

F32 = jnp.float32
BF16 = jnp.bfloat16

D_MODEL = 4096
DEPTH = 2
ATTN_WIDTH = 2048
GMLP_WIDTH = 2048
HEAD_DIM = 128
V_DIM = 2 * HEAD_DIM
N_HEADS = ATTN_WIDTH // V_DIM
N_GROUPS = 8
GROUP_DIM = GMLP_WIDTH // N_GROUPS
CHUNK = 128
N_BUCKETS = 32
MAX_DISTANCE = 128
NORM_EPS = 1e-6
N_MOD = 6
QKV_COLS = 3 * ATTN_WIDTH

V7X_VMEM_BYTES = 64 * 1024 * 1024
VMEM_LIMIT_BYTES = V7X_VMEM_BYTES - 4 * 1024 * 1024
SUBLANES = 8
BF16_SUBLANES = 16
LANES = 128
LOG2E = math.log2(math.e)

MATMUL_TILE = 1024
MATMUL_K_TILE = 4096
ADA_TILE = 1024
ADA_HOSTED_TILE = 256
ATTN_BLOCK = 512
ROW_BLOCK = 256
GATE_BLOCK = 512
NORM_SLAB = 64
NORM_COLS = 512


def _params(semantics):
    return pltpu.CompilerParams(dimension_semantics=semantics,
                                vmem_limit_bytes=VMEM_LIMIT_BYTES)


def _ada_slab(c_ref, w_ref, b_ref):
    c = c_ref[...]
    c_act = (c * jax.nn.sigmoid(c)).astype(BF16)
    return jnp.dot(c_act, w_ref[...].astype(BF16), preferred_element_type=F32) + b_ref[...]


def _ada_kernel(c_ref, w_ref, b_ref, o_ref):
    o_ref[...] = _ada_slab(c_ref, w_ref, b_ref)


def _ada_mod(c_pad, w_ada, b_ada3, layer):
    D = c_pad.shape[1]
    N = w_ada.shape[2]
    tn = ADA_TILE
    return pl.pallas_call(
        _ada_kernel,
        grid=(N // tn,),
        in_specs=[
            pl.BlockSpec((SUBLANES, D), lambda j: (0, 0)),
            pl.BlockSpec((None, D, tn), lambda j: (layer, 0, j)),
            pl.BlockSpec((None, 1, tn), lambda j: (layer, 0, j)),
        ],
        out_specs=pl.BlockSpec((SUBLANES, tn), lambda j: (0, j)),
        out_shape=jax.ShapeDtypeStruct((SUBLANES, N), F32),
        compiler_params=_params(("arbitrary",)),
        name="ada_mod",
    )(c_pad, w_ada, b_ada3)


def _cast_slab_specs(cast_src, grid, n_slabs=None):
    stacked, layer = cast_src
    n_steps = math.prod(grid)
    n_slabs = n_steps if n_slabs is None else n_slabs
    _, rows, cols = stacked.shape
    assert rows % (n_slabs * BF16_SUBLANES) == 0 and n_slabs <= n_steps
    slab = rows // n_slabs
    strides = [math.prod(grid[d + 1:]) for d in range(len(grid))]

    def step(*g):
        t = sum(gi * st for gi, st in zip(g, strides))
        return t if n_slabs == n_steps else jnp.minimum(t, n_slabs - 1)

    return (pl.BlockSpec((None, slab, cols), lambda *g: (layer, step(*g), 0)),
            pl.BlockSpec((slab, cols), lambda *g: (step(*g), 0)),
            jax.ShapeDtypeStruct((rows, cols), BF16))


def _rms(x, g):
    ms = jnp.mean(x * x, axis=-1, keepdims=True)
    return x * lax.rsqrt(ms + NORM_EPS) * g


def _prenorm_kernel(x_ref, g_ref, mod_ref, cast_in_ref, h_ref, cast_out_ref, *, shift_idx):
    cast_out_ref[...] = cast_in_ref[...].astype(BF16)
    shift = mod_ref[0, shift_idx:shift_idx + 1, :]
    scale = mod_ref[0, shift_idx + 1:shift_idx + 2, :]
    h = _rms(x_ref[...], g_ref[...]) * (1.0 + scale) + shift
    h_ref[...] = h.astype(BF16)


def _prenorm(x2, g, mod, shift_idx, rows_per_batch, cast_src):
    M, D = x2.shape
    tm = ROW_BLOCK
    bpb = rows_per_batch // tm
    grid = (M // tm,)
    cast_in_spec, cast_out_spec, cast_shape = _cast_slab_specs(cast_src, grid)
    return pl.pallas_call(
        functools.partial(_prenorm_kernel, shift_idx=shift_idx),
        grid=grid,
        in_specs=[
            pl.BlockSpec((tm, D), lambda i: (i, 0)),
            pl.BlockSpec((1, D), lambda i: (0, 0)),
            pl.BlockSpec((1, N_MOD, D), lambda i: (i // bpb, 0, 0)),
            cast_in_spec,
        ],
        out_specs=[pl.BlockSpec((tm, D), lambda i: (i, 0)), cast_out_spec],
        out_shape=[jax.ShapeDtypeStruct((M, D), BF16), cast_shape],
        compiler_params=_params(("arbitrary",)),
        name="prenorm",
    )(x2, g.reshape(1, D), mod, cast_src[0])


def _postnorm_kernel(y_ref, x_ref, pg_ref, mod_ref, ng_ref, xo_ref, h_ref=None, *,
                     gate_idx, next_shift_idx):
    gate = mod_ref[0, gate_idx:gate_idx + 1, :]
    x_new = x_ref[...] + gate * _rms(y_ref[...], pg_ref[...])
    xo_ref[...] = x_new
    if next_shift_idx is not None:
        shift = mod_ref[1, next_shift_idx:next_shift_idx + 1, :]
        scale = mod_ref[1, next_shift_idx + 1:next_shift_idx + 2, :]
        h = _rms(x_new, ng_ref[...]) * (1.0 + scale) + shift
        h_ref[...] = h.astype(BF16)


def _postnorm(y2, x2, post_g, mod_pair, next_g, gate_idx, next_shift_idx, rows_per_batch):
    M, D = x2.shape
    tm = ROW_BLOCK
    bpb = rows_per_batch // tm
    row_spec = pl.BlockSpec((tm, D), lambda i: (i, 0))
    vec_spec = pl.BlockSpec((1, D), lambda i: (0, 0))
    with_next = next_shift_idx is not None
    kern = functools.partial(_postnorm_kernel, gate_idx=gate_idx,
                             next_shift_idx=next_shift_idx)
    out_shape = [jax.ShapeDtypeStruct((M, D), F32)]
    out_specs = [row_spec]
    if with_next:
        out_shape.append(jax.ShapeDtypeStruct((M, D), BF16))
        out_specs.append(row_spec)
    res = pl.pallas_call(
        kern,
        grid=(M // tm,),
        in_specs=[
            row_spec, row_spec, vec_spec,
            pl.BlockSpec((None, 2, N_MOD, D), lambda i: (i // bpb, 0, 0, 0)),
            vec_spec,
        ],
        out_specs=out_specs,
        out_shape=out_shape,
        compiler_params=_params(("arbitrary",)),
        name="postnorm",
    )(y2, x2, post_g.reshape(1, D), mod_pair, next_g.reshape(1, D))
    return (res[0], res[1]) if with_next else (res[0], None)


def _norm_matmul_kernel(y_ref, x_ref, pg_ref, mod_ref, ng_ref, w_ref, *rest, n_i, n_cast,
                        gate_idx, next_shift_idx, epilogue):
    cast_in = rest[:n_cast]
    o_ref, xo_ref = rest[n_cast], rest[n_cast + 1]
    cast_out = rest[n_cast + 2:2 * n_cast + 2]
    h_refs = rest[2 * n_cast + 2:]
    i = pl.program_id(0)
    j = pl.program_id(1)

    def norm_slab(h_ref):
        for src, dst in zip(cast_in, cast_out):
            dst[...] = src[...].astype(BF16)
        D = y_ref.shape[1]
        chunks = [slice(c0, c0 + NORM_COLS) for c0 in range(0, D, NORM_COLS)]
        rows = pl.ds(pl.multiple_of(j * NORM_SLAB, NORM_SLAB), NORM_SLAB)

        def lane_partial(v):
            out = v[:, :LANES]
            for t in range(1, NORM_COLS // LANES):
                out = out + v[:, t * LANES:(t + 1) * LANES]
            return out

        def inv_rms(partials):
            ms = jnp.sum(sum(partials), axis=-1, keepdims=True) / D
            return lax.rsqrt(ms + NORM_EPS)

        r_y = inv_rms([lane_partial(y_ref[:, cs] * y_ref[:, cs]) for cs in chunks])
        partials = []
        for cs in chunks:
            gate = mod_ref[0, gate_idx:gate_idx + 1, cs]
            x_new = x_ref[:, cs] + gate * (y_ref[:, cs] * r_y * pg_ref[:, cs])
            xo_ref[:, cs] = x_new
            partials.append(lane_partial(x_new * x_new))
        r_x = inv_rms(partials)
        for cs in chunks:
            shift = mod_ref[1, next_shift_idx:next_shift_idx + 1, cs]
            scale = mod_ref[1, next_shift_idx + 1:next_shift_idx + 2, cs]
            h = xo_ref[:, cs] * r_x * ng_ref[:, cs] * (1.0 + scale) + shift
            h_ref[rows, cs] = h.astype(BF16)

    def matmul(h_ref):
        r = jnp.dot(h_ref[...], w_ref[...], preferred_element_type=F32)
        if epilogue == "relu2":
            r = jnp.square(jnp.maximum(r, 0.0))
        o_ref[...] = r.astype(o_ref.dtype)

    @pl.when(i == 0)
    def _():
        norm_slab(h_refs[0])

    for parity in (0, 1):
        @pl.when(jnp.logical_and(jnp.logical_and(i >= 1, i < n_i), i % 2 == parity))
        def _():
            norm_slab(h_refs[parity])
            matmul(h_refs[1 - parity])

    @pl.when(i == n_i)
    def _():
        matmul(h_refs[(n_i - 1) % 2])


def _norm_matmul(y2, x2, post_g, mod_pair, next_g, gate_idx, next_shift_idx, rows_per_batch,
                 w, *, out_dtype, epilogue=None, cast_srcs=(), name="norm_matmul"):
    M, D = x2.shape
    N = w.shape[1]
    tm = tn = MATMUL_TILE
    n_i, n_j = M // tm, N // tn
    assert M % tm == 0 and N % tn == 0 and tm == n_j * NORM_SLAB
    n_slabs = n_i * n_j
    grid = (n_i + 1, n_j)
    bpb = rows_per_batch // tm

    def slab(i, j):
        return (jnp.minimum(i * n_j + j, n_slabs - 1), 0)

    def col(i, j):
        return jnp.where(i == 0, 0, j)

    slab_spec = pl.BlockSpec((NORM_SLAB, D), slab)
    vec_spec = pl.BlockSpec((1, D), lambda i, j: (0, 0))
    in_specs = [
        slab_spec, slab_spec, vec_spec,
        pl.BlockSpec((None, 2, N_MOD, D),
                     lambda i, j: (jnp.minimum(i, n_i - 1) // bpb, 0, 0, 0)),
        vec_spec,
        pl.BlockSpec((D, tn), lambda i, j: (0, col(i, j))),
    ]
    out_specs = [pl.BlockSpec((tm, tn), lambda i, j: (jnp.maximum(i - 1, 0), col(i, j))),
                 slab_spec]
    out_shape = [jax.ShapeDtypeStruct((M, N), out_dtype), jax.ShapeDtypeStruct((M, D), F32)]
    operands = [y2, x2, post_g.reshape(1, D), mod_pair, next_g.reshape(1, D), w]
    for src in cast_srcs:
        cast_in_spec, cast_out_spec, cast_shape = _cast_slab_specs(src, grid, n_slabs)
        in_specs.append(cast_in_spec)
        out_specs.append(cast_out_spec)
        out_shape.append(cast_shape)
        operands.append(src[0])
    return pl.pallas_call(
        functools.partial(_norm_matmul_kernel, n_i=n_i, n_cast=len(cast_srcs),
                          gate_idx=gate_idx, next_shift_idx=next_shift_idx,
                          epilogue=epilogue),
        grid=grid,
        in_specs=in_specs,
        out_specs=out_specs,
        out_shape=out_shape,
        scratch_shapes=[pltpu.VMEM((tm, D), BF16), pltpu.VMEM((tm, D), BF16)],
        compiler_params=_params(("arbitrary", "arbitrary")),
        name=name,
    )(*operands)


def _matmul_kernel(*refs, n_lhs, k_sizes, n_k, epilogue, n_cast):
    lhs_refs = refs[:n_lhs]
    w_ref = refs[n_lhs]
    n_in = n_lhs + 1 + n_cast
    o_ref = refs[n_in]
    for t in range(n_cast):
        refs[n_in + 1 + t][...] = refs[n_lhs + 1 + t][...].astype(BF16)
    acc = None
    off = 0
    for a_ref, ks in zip(lhs_refs, k_sizes):
        part = jnp.dot(a_ref[...], w_ref[off:off + ks, :], preferred_element_type=F32)
        acc = part if acc is None else acc + part
        off += ks

    if n_k == 1:
        if epilogue == "relu2":
            acc = jnp.square(jnp.maximum(acc, 0.0))
        o_ref[...] = acc.astype(o_ref.dtype)
    else:
        k = pl.program_id(2)

        @pl.when(k == 0)
        def _():
            o_ref[...] = acc

        @pl.when(k > 0)
        def _():
            o_ref[...] += acc


def _matmul(lhs_list, w, *, out_dtype, tk=None, col_offset=0, n_cols=None,
            epilogue=None, cast_srcs=(), name="matmul"):
    M = lhs_list[0].shape[0]
    K, N_total = w.shape
    N = N_total - col_offset if n_cols is None else n_cols
    tm = tn = MATMUL_TILE
    k_sizes = tuple(a.shape[1] for a in lhs_list)
    assert sum(k_sizes) == K and M % tm == 0 and N % tn == 0 and col_offset % tn == 0
    joff = col_offset // tn
    if tk is None:
        n_k = 1
        in_specs = [pl.BlockSpec((tm, ks), lambda i, j: (i, 0)) for ks in k_sizes]
        in_specs.append(pl.BlockSpec((K, tn), lambda i, j: (0, j + joff)))
        out_specs = [pl.BlockSpec((tm, tn), lambda i, j: (i, j))]
        grid = (M // tm, N // tn)
        kern_k_sizes = k_sizes
    else:
        assert len(lhs_list) == 1 and K % tk == 0 and out_dtype == F32 and epilogue is None
        n_k = K // tk
        in_specs = [pl.BlockSpec((tm, tk), lambda i, j, k: (i, k)),
                    pl.BlockSpec((tk, tn), lambda i, j, k: (k, j + joff))]
        out_specs = [pl.BlockSpec((tm, tn), lambda i, j, k: (i, j))]
        grid = (M // tm, N // tn, n_k)
        kern_k_sizes = (tk,)
    out_shape = [jax.ShapeDtypeStruct((M, N), out_dtype)]
    operands = list(lhs_list) + [w]
    for src in cast_srcs:
        cast_in_spec, cast_out_spec, cast_shape = _cast_slab_specs(src, grid)
        in_specs.append(cast_in_spec)
        out_specs.append(cast_out_spec)
        out_shape.append(cast_shape)
        operands.append(src[0])
    res = pl.pallas_call(
        functools.partial(_matmul_kernel, n_lhs=len(lhs_list), k_sizes=kern_k_sizes,
                          n_k=n_k, epilogue=epilogue, n_cast=len(cast_srcs)),
        grid=grid,
        in_specs=in_specs,
        out_specs=out_specs,
        out_shape=out_shape,
        compiler_params=_params(("arbitrary",) * len(grid)),
        name=name,
    )(*operands)
    return res if cast_srcs else res[0]


def _bucket_thresholds():
    max_exact = N_BUCKETS // 2
    ths = []
    for b in range(max_exact + 1, N_BUCKETS):
        n = max_exact
        while True:
            v = max_exact + int(math.log(n / max_exact) / math.log(MAX_DISTANCE / max_exact)
                                * (N_BUCKETS - max_exact))
            if min(v, N_BUCKETS - 1) >= b:
                break
            n += 1
        ths.append(n)
    return ths


def _bias_kernel(rb_ref, o_ref):
    h = pl.program_id(0)
    T, C = ATTN_BLOCK, MAX_DISTANCE
    row = lax.broadcasted_iota(jnp.int32, (C, C), 0)
    col = lax.broadcasted_iota(jnp.int32, (C, C), 1)
    far = rb_ref[N_BUCKETS - 1, h]

    def band_tile(offset):
        d = row - col + offset
        n = jnp.maximum(d, 0)
        bucket = jnp.minimum(n, N_BUCKETS // 2)
        for th in _bucket_thresholds():
            bucket = bucket + (n >= th).astype(jnp.int32)
        tile = jnp.zeros((C, C), F32)
        for b in range(N_BUCKETS - 1):
            tile = jnp.where(bucket == b, (rb_ref[b, h] - far) * LOG2E, tile)
        return jnp.where(d >= 0, tile, -jnp.inf)

    on_diag = band_tile(0)
    below_diag = band_tile(C)
    zeros = jnp.zeros((C, C), F32)
    future = jnp.full((C, C), -jnp.inf, F32)
    nb = T // C
    for i in range(nb):
        for j in range(nb):
            sub = on_diag if i == j else below_diag if i == j + 1 else zeros if i > j else future
            o_ref[0, 0, i * C:(i + 1) * C, j * C:(j + 1) * C] = sub
            near = i == 0 and j == nb - 1
            o_ref[0, 1, i * C:(i + 1) * C, j * C:(j + 1) * C] = below_diag if near else zeros


def _bias_tiles(rel_bias):
    T = ATTN_BLOCK
    return pl.pallas_call(
        _bias_kernel,
        grid=(N_HEADS,),
        in_specs=[pl.BlockSpec(memory_space=pltpu.SMEM)],
        out_specs=pl.BlockSpec((1, 2, T, T), lambda h: (h, 0, 0, 0)),
        out_shape=jax.ShapeDtypeStruct((N_HEADS, 2, T, T), F32),
        compiler_params=_params(("arbitrary",)),
        name="bias_tiles",
    )(rel_bias)


def _lane_tile(x, n):
    return x if n == 1 else jnp.concatenate([x] * n, axis=1)


def _attn_kernel(q_ref, k_ref, v_ref, bias_ref, lam_ref, g_ref, cast_in_ref, *rest,
                 lambda_init, n_ada):
    T = ATTN_BLOCK
    qi = pl.program_id(2)
    scale2 = HEAD_DIM ** -0.5 * LOG2E

    if n_ada:
        c_ref, wada_ref, bada_ref, o_ref, cast_out_ref, mod_ref = rest[:6]

        @pl.when(qi >= pl.num_programs(2) - n_ada)
        def _():
            mod_ref[...] = _ada_slab(c_ref, wada_ref, bada_ref)
    else:
        o_ref, cast_out_ref = rest[:2]
    m_ref, l_ref, acc_ref, sa_ref, sb_ref = rest[-5:]

    cast_out_ref[...] = cast_in_ref[...].astype(BF16)
    m_ref[...] = jnp.full(m_ref.shape, -jnp.inf, F32)
    l_ref[...] = jnp.zeros(l_ref.shape, F32)
    acc_ref[...] = jnp.zeros(acc_ref.shape, F32)

    def scores(j, s_ref):
        start = pl.multiple_of(j * T, T)
        k = k_ref[pl.ds(start, T), :]
        for c in range(2):
            lo, hi = c * HEAD_DIM, (c + 1) * HEAD_DIM
            s_ref[c] = lax.dot_general(
                q_ref[:, lo:hi], k[:, lo:hi], (((1,), (1,)), ((), ())),
                preferred_element_type=F32) * scale2

    def consume(j, s_ref, bias):
        start = pl.multiple_of(j * T, T)
        v = v_ref[pl.ds(start, T), :]
        for c in range(2):
            s = s_ref[c]
            if bias is not None:
                s = s + bias
            m_old = m_ref[c]
            m_new = jnp.maximum(m_old, jnp.max(s, axis=-1, keepdims=True))
            alpha = jnp.exp2(m_old - m_new)
            p = jnp.exp2(s - _lane_tile(m_new, T // LANES))
            p_part = p[:, :LANES]
            for t in range(1, T // LANES):
                p_part = p_part + p[:, t * LANES:(t + 1) * LANES]
            l_ref[c] = alpha * l_ref[c] + p_part
            acc_ref[c] = (_lane_tile(alpha, V_DIM // LANES) * acc_ref[c]
                          + jnp.dot(p.astype(BF16), v, preferred_element_type=F32))
            m_ref[c] = m_new

    scores(0, sa_ref)

    def plain_pair(i, carry):
        j = 2 * i
        scores(j + 1, sb_ref)
        consume(j, sa_ref, None)
        scores(j + 2, sa_ref)
        consume(j + 1, sb_ref, None)
        return carry

    n_plain = jnp.maximum(qi - 1, 0)
    lax.fori_loop(0, n_plain // 2, plain_pair, 0)

    @pl.when(qi == 0)
    def _():
        consume(0, sa_ref, bias_ref[0, 0])

    @pl.when(qi % 2 == 1)
    def _():
        scores(qi, sb_ref)
        consume(qi - 1, sa_ref, bias_ref[0, 1])
        consume(qi, sb_ref, bias_ref[0, 0])

    @pl.when(jnp.logical_and(qi % 2 == 0, qi >= 2))
    def _():
        scores(qi - 1, sb_ref)
        consume(qi - 2, sa_ref, None)
        scores(qi, sa_ref)
        consume(qi - 1, sb_ref, bias_ref[0, 1])
        consume(qi, sa_ref, bias_ref[0, 0])

    lam_v = lam_ref[...]
    lam = (jnp.exp(jnp.sum(lam_v[0:1] * lam_v[1:2], axis=-1, keepdims=True))
           - jnp.exp(jnp.sum(lam_v[2:3] * lam_v[3:4], axis=-1, keepdims=True))
           + lambda_init)
    l0 = jnp.sum(l_ref[0], axis=-1, keepdims=True)
    l1 = jnp.sum(l_ref[1], axis=-1, keepdims=True)
    out = acc_ref[0] / l0 - lam * (acc_ref[1] / l1)
    o_ref[...] = (_rms(out, g_ref[...]) * (1.0 - lambda_init)).astype(o_ref.dtype)


def _attention(qkv, bias_tiles, lam_vecs, subln_g, lambda_init, B, S, cast_src, ada=None):
    T = ATTN_BLOCK
    nq = S // T
    grid = (B, N_HEADS, nq)
    cast_in_spec, cast_out_spec, cast_shape = _cast_slab_specs(cast_src, grid)
    in_specs = [
        pl.BlockSpec((T, V_DIM), lambda b, h, i: (b * nq + i, h)),
        pl.BlockSpec((S, V_DIM), lambda b, h, i: (b, N_HEADS + h)),
        pl.BlockSpec((S, V_DIM), lambda b, h, i: (b, 2 * N_HEADS + h)),
        pl.BlockSpec((1, 2, T, T), lambda b, h, i: (h, 0, 0, 0)),
        pl.BlockSpec((4, HEAD_DIM), lambda b, h, i: (0, 0)),
        pl.BlockSpec((1, V_DIM), lambda b, h, i: (0, 0)),
        cast_in_spec,
    ]
    out_specs = [pl.BlockSpec((T, V_DIM), lambda b, h, i: (b * nq + i, h)), cast_out_spec]
    out_shape = [jax.ShapeDtypeStruct((B * S, ATTN_WIDTH), BF16), cast_shape]
    operands = [qkv, qkv, qkv, bias_tiles, lam_vecs, subln_g.reshape(1, V_DIM), cast_src[0]]
    n_ada = 0
    if ada is not None:
        c_pad, w_ada, b_ada3, layer = ada
        D, n_mod = w_ada.shape[1:]
        tn = ADA_HOSTED_TILE
        n_ada = n_mod // (tn * B * N_HEADS)
        assert n_mod == n_ada * tn * B * N_HEADS and n_ada <= nq

        def col(b, h, i):
            return (b * N_HEADS + h) * n_ada + jnp.maximum(i - (nq - n_ada), 0)

        in_specs += [
            pl.BlockSpec((SUBLANES, D), lambda b, h, i: (0, 0)),
            pl.BlockSpec((None, D, tn), lambda b, h, i: (layer, 0, col(b, h, i))),
            pl.BlockSpec((None, 1, tn), lambda b, h, i: (layer, 0, col(b, h, i))),
        ]
        out_specs.append(pl.BlockSpec((SUBLANES, tn), lambda b, h, i: (0, col(b, h, i))))
        out_shape.append(jax.ShapeDtypeStruct((SUBLANES, n_mod), F32))
        operands += [c_pad, w_ada, b_ada3]
    return pl.pallas_call(
        functools.partial(_attn_kernel, lambda_init=lambda_init, n_ada=n_ada),
        grid=grid,
        in_specs=in_specs,
        out_specs=out_specs,
        out_shape=out_shape,
        scratch_shapes=[pltpu.VMEM((2, T, LANES), F32), pltpu.VMEM((2, T, LANES), F32),
                        pltpu.VMEM((2, T, V_DIM), F32), pltpu.VMEM((2, T, T), F32),
                        pltpu.VMEM((2, T, T), F32)],
        compiler_params=_params(("arbitrary", "arbitrary", "arbitrary")),
        name="diff_attention",
    )(*operands)


def _gate_kernel(z_ref, ws_ref, bs_ref, vg_ref, vb_ref, o_ref):
    z = z_ref[...]
    a = 0.5 * z * (1.0 + lax.erf(z * math.sqrt(0.5)))
    u = a[:, :GMLP_WIDTH]
    v = a[:, GMLP_WIDTH:]
    mu = jnp.mean(v, axis=-1, keepdims=True)
    vc = v - mu
    var = jnp.mean(vc * vc, axis=-1, keepdims=True)
    vn = (vc * lax.rsqrt(var + NORM_EPS) * vg_ref[...] + vb_ref[...]).astype(BF16)
    row = lax.broadcasted_iota(jnp.int32, (CHUNK, CHUNK), 0)
    col = lax.broadcasted_iota(jnp.int32, (CHUNK, CHUNK), 1)
    causal = row >= col
    bs = bs_ref[...]
    for g in range(N_GROUPS):
        ws = jnp.where(causal, ws_ref[g], 0.0).astype(BF16)
        cols = slice(g * GROUP_DIM, (g + 1) * GROUP_DIM)
        for n in range(z.shape[0] // CHUNK):
            rows = slice(n * CHUNK, (n + 1) * CHUNK)
            mixed = jnp.dot(ws, vn[rows, cols], preferred_element_type=F32) + bs[:, g:g + 1]
            o_ref[rows, cols] = (u[rows, cols] * mixed).astype(o_ref.dtype)


def _spatial_gate(z2, w_s, b_s, v_g, v_b):
    M = z2.shape[0]
    R = GATE_BLOCK
    return pl.pallas_call(
        _gate_kernel,
        grid=(M // R,),
        in_specs=[
            pl.BlockSpec((R, 2 * GMLP_WIDTH), lambda i: (i, 0)),
            pl.BlockSpec((N_GROUPS, CHUNK, CHUNK), lambda i: (0, 0, 0)),
            pl.BlockSpec((CHUNK, N_GROUPS), lambda i: (0, 0)),
            pl.BlockSpec((1, GMLP_WIDTH), lambda i: (0, 0)),
            pl.BlockSpec((1, GMLP_WIDTH), lambda i: (0, 0)),
        ],
        out_specs=pl.BlockSpec((R, GMLP_WIDTH), lambda i: (i, 0)),
        out_shape=jax.ShapeDtypeStruct((M, GMLP_WIDTH), BF16),
        compiler_params=_params(("arbitrary",)),
        name="spatial_gate",
    )(z2, w_s, b_s.T, v_g.reshape(1, GMLP_WIDTH), v_b.reshape(1, GMLP_WIDTH))


def kernel(x, c, rel_bias, w_ada, b_ada, pre_mix_g, w_in, lambda_q1, lambda_k1,
           lambda_q2, lambda_k2, subln_g, v_norm_g, v_norm_b, w_s, b_s, w_out,
           post_mix_g, pre_mlp_g, w_1, w_2, post_mlp_g):
    B, S, D = x.shape
    assert (D, x.dtype) == (D_MODEL, F32) and S % ATTN_BLOCK == 0 and S % ROW_BLOCK == 0
    M = B * S
    c_pad = jnp.zeros((SUBLANES, D), F32).at[:B].set(c)
    b_ada3 = b_ada.reshape(DEPTH, 1, N_MOD * D)

    def mod_rows(raw):
        return raw[:B].reshape(B, N_MOD, D)

    mod = mod_rows(_ada_mod(c_pad, w_ada, b_ada3, 0))
    bias_tiles = _bias_tiles(rel_bias)
    x2 = x.reshape(M, D)
    h, w_in_b = _prenorm(x2, pre_mix_g[0], mod, 0, S, cast_src=(w_in, 0))
    for l in range(DEPTH):
        last = l + 1 == DEPTH
        lambda_init = 0.8 - 0.6 * math.exp(-0.3 * l)
        qkv = _matmul([h], w_in_b, out_dtype=BF16, n_cols=QKV_COLS,
                      name="in_proj_qkv")
        z, w_out_b = _matmul([h], w_in_b, out_dtype=F32,
                             col_offset=QKV_COLS, cast_srcs=((w_out, l),), name="in_proj_z")
        lam_vecs = jnp.stack([lambda_q1[l], lambda_k1[l], lambda_q2[l], lambda_k2[l]])
        attn, w_1_b, *mod_next_raw = _attention(
            qkv, bias_tiles, lam_vecs, subln_g[l], lambda_init, B, S, cast_src=(w_1, l),
            ada=None if last else (c_pad, w_ada, b_ada3, l + 1))
        gm = _spatial_gate(z, w_s[l], b_s[l], v_norm_g[l], v_norm_b[l])
        y = _matmul([attn, gm], w_out_b, out_dtype=F32, name="out_proj")
        mod_same = jnp.stack([mod, mod], axis=1)
        a, x2, w_2_b = _norm_matmul(y, x2, post_mix_g[l], mod_same, pre_mlp_g[l], 2, 3, S,
                                    w_1_b, out_dtype=BF16,
                                    epilogue="relu2", cast_srcs=((w_2, l),), name="mlp_up")
        if last:
            y = _matmul([a], w_2_b, out_dtype=F32, tk=MATMUL_K_TILE,
                        name="mlp_down")
            x2, _ = _postnorm(y, x2, post_mlp_g[l], mod_same, post_mlp_g[l], 5, None, S)
        else:
            y, w_in_b = _matmul([a], w_2_b, out_dtype=F32, tk=MATMUL_K_TILE,
                                cast_srcs=((w_in, l + 1),), name="mlp_down")
            mod_next = mod_rows(mod_next_raw[0])
            mod_pair = jnp.stack([mod, mod_next], axis=1)
            x2, h = _postnorm(y, x2, post_mlp_g[l], mod_pair, pre_mix_g[l + 1], 5, 0, S)
            mod = mod_next
    return x2.reshape(B, S, D)
```
